```python
import jax, jax.numpy as jnp
from jax import lax
import numpy as np

D_MODEL = 2048
BATCH = 4
SEQ = 8192
DEPTH = 1
DEC_BATCH = 32
DEC_SEQ = 64
PAST_LEN = 1024

CHUNK = 64
D_MIX = D_MODEL
D_A = D_MIX // 2
D_B = D_MIX - D_A
SGU_CHUNK = 128
A_HEADS = 8
A_HEAD_DIM = D_A // A_HEADS
B_HEAD_DIM = 64
B_HEADS = D_B // B_HEAD_DIM
LORA_W = 64
LORA_A = 64
N_SHIFT = 3 * D_B + LORA_W + LORA_A
D_PROJ = 3 * D_A + N_SHIFT + D_B
NORM_EPS = 1e-6
LN_EPS = 1e-5
GN_EPS = 64e-5
W_OFFSET = 0.5

kernel_name = "hybrid_gmlp_rwkv7_stream_step"


def rms_norm(x, g):
    xf = x.astype(jnp.float32)
    y = xf * lax.rsqrt(jnp.mean(xf * xf, axis=-1, keepdims=True) + NORM_EPS)
    return (y * g.astype(jnp.float32)).astype(x.dtype)


def layer_norm(x, g, b):
    xf = x.astype(jnp.float32)
    mu = jnp.mean(xf, axis=-1, keepdims=True)
    var = jnp.mean(jnp.square(xf - mu), axis=-1, keepdims=True)
    y = (xf - mu) * lax.rsqrt(var + LN_EPS)
    return (y * g.astype(jnp.float32) + b.astype(jnp.float32)).astype(x.dtype)


def token_shift(z, prev, mu):
    z_prev = jnp.concatenate([prev.astype(z.dtype), z[:, :-1]], axis=1)
    return z + mu * (z_prev - z)


def sgu_mix(vn, w_s, b_s):
    L = vn.shape[2]
    mask = jnp.tril(jnp.ones((L, L), dtype=bool))
    w = jnp.where(mask[None], w_s[:, :L, :L], jnp.zeros((), w_s.dtype)).astype(vn.dtype)
    f = jnp.einsum('hij,bcjhd->bcihd', w, vn)
    return f + b_s[:, :L].T[None, None, :, :, None].astype(vn.dtype)


def wkv7_scan(S0, r, w, k, v, kk, b):
    def step(S, inp):
        r_t, w_t, k_t, v_t, kk_t, b_t = inp
        sa = -jnp.einsum('bhvk,bhk->bhv', S, kk_t)
        S = (S * w_t[:, :, None, :] + sa[..., None] * b_t[:, :, None, :]
             + v_t[..., None] * k_t[:, :, None, :])
        y = jnp.einsum('bhvk,bhk->bhv', S, r_t)
        return S, y
    xs = tuple(jnp.swapaxes(t, 0, 1) for t in (r, w, k, v, kk, b))
    S, ys = lax.scan(step, S0, xs)
    return jnp.swapaxes(ys, 0, 1), S


def hybrid_layer(x, wkv0, shift0, norm_g, w_in, w_out, sgu_ln_g, sgu_ln_b, sgu_w, sgu_b,
                 shift_mu, w0, w2, a0, a2, k_k, k_a, r_k, gn_g, gn_b):
    bsz, t, _ = x.shape
    f32 = jnp.float32
    h = rms_norm(x, norm_g)
    z = h @ w_in
    za, zs, g_b = jnp.split(z, [3 * D_A, 3 * D_A + N_SHIFT], axis=-1)

    u, v, g_a = jnp.split(za, 3, axis=-1)
    vn = layer_norm(jax.nn.gelu(v, approximate=False), sgu_ln_g, sgu_ln_b)
    L = min(t, SGU_CHUNK)
    f = sgu_mix(vn.reshape(bsz, t // L, L, A_HEADS, A_HEAD_DIM), sgu_w, sgu_b).reshape(bsz, t, D_A)
    out_a = jax.nn.gelu(u, approximate=False) * f * jax.nn.silu(g_a)

    new_shift = zs[:, -1:]
    zs_mix = token_shift(zs, shift0, shift_mu).astype(f32)
    r, k, vb, hw, ha = jnp.split(zs_mix, [D_B, 2 * D_B, 3 * D_B, 3 * D_B + LORA_W], axis=-1)
    d = w0.astype(f32) + jnp.tanh(hw) @ w2.astype(f32)
    decay = jnp.exp(-jnp.exp(-jax.nn.softplus(-d) - W_OFFSET))
    a = jax.nn.sigmoid(a0.astype(f32) + ha @ a2.astype(f32))
    hs = (bsz, t, B_HEADS, B_HEAD_DIM)
    kk = (k * k_k.astype(f32)).reshape(hs)
    kk = kk / jnp.maximum(jnp.linalg.norm(kk, axis=-1, keepdims=True), 1e-12)
    k = k * (1.0 + (a - 1.0) * k_a.astype(f32))
    r_h, k_h, v_h, a_h = r.reshape(hs), k.reshape(hs), vb.reshape(hs), a.reshape(hs)
    y, S = wkv7_scan(wkv0.astype(f32), r_h, decay.reshape(hs), k_h, v_h, kk, kk * a_h)
    mu = jnp.mean(y, axis=-1, keepdims=True)
    var = jnp.mean(jnp.square(y - mu), axis=-1, keepdims=True)
    y = ((y - mu) * lax.rsqrt(var + GN_EPS)).reshape(bsz, t, D_B)
    y = y * gn_g.astype(f32) + gn_b.astype(f32)
    bonus = jnp.sum(r_h * k_h * r_k.astype(f32), axis=-1, keepdims=True) * v_h
    y = y + bonus.reshape(bsz, t, D_B)
    out_b = y.astype(x.dtype) * jax.nn.silu(g_b)

    o = jnp.concatenate([out_a, out_b], axis=-1) @ w_out
    return x + o, S, new_shift, vn


def setup_inputs(seed: int = 0) -> dict:
    key = jax.random.key(seed)
    ks = jax.random.split(key, 24)
    nrm = lambda k, s, sc: jax.random.normal(k, s, jnp.float32) * sc
    return {
        "x_prompt": nrm(ks[0], (BATCH, SEQ, D_MODEL), 1.0),
        "x_sample": nrm(ks[1], (DEC_BATCH, DEC_SEQ, D_MODEL), 1.0),
        "state_b_wkv": nrm(ks[2], (DEPTH, DEC_BATCH, B_HEADS, B_HEAD_DIM, B_HEAD_DIM), 0.5),
        "state_b_shift": nrm(ks[3], (DEPTH, DEC_BATCH, 1, N_SHIFT), 1.0),
        "norm_g": 1.0 + nrm(ks[4], (DEPTH, D_MODEL), 0.02),
        "w_in": nrm(ks[5], (DEPTH, D_MODEL, D_PROJ), D_MODEL ** -0.5),
        "w_out": nrm(ks[6], (DEPTH, D_MIX, D_MODEL), 0.5 * D_MIX ** -0.5),
        "sgu_ln_g": 1.0 + nrm(ks[7], (DEPTH, D_A), 0.02),
        "sgu_ln_b": nrm(ks[8], (DEPTH, D_A), 0.02),
        "sgu_w": nrm(ks[9], (DEPTH, A_HEADS, SGU_CHUNK, SGU_CHUNK), SGU_CHUNK ** -0.5),
        "sgu_b": 1.0 + nrm(ks[10], (DEPTH, A_HEADS, SGU_CHUNK), 0.1),
        "shift_mu": jax.random.uniform(ks[11], (DEPTH, N_SHIFT), jnp.float32),
        "w0": -3.0 + nrm(ks[12], (DEPTH, D_B), 1.0),
        "w2": nrm(ks[13], (DEPTH, LORA_W, D_B), 0.5 * LORA_W ** -0.5),
        "a0": nrm(ks[14], (DEPTH, D_B), 0.5),
        "a2": nrm(ks[15], (DEPTH, LORA_A, D_B), 0.5 * LORA_A ** -0.5),
        "k_k": 0.85 + nrm(ks[16], (DEPTH, D_B), 0.05),
        "k_a": 1.0 + nrm(ks[17], (DEPTH, D_B), 0.05),
        "r_k": nrm(ks[18], (DEPTH, B_HEADS, B_HEAD_DIM), 0.1),
        "gn_g": 1.0 + nrm(ks[19], (DEPTH, D_B), 0.02),
        "gn_b": nrm(ks[20], (DEPTH, D_B), 0.02),
        "final_g": 1.0 + nrm(ks[21], (D_MODEL,), 0.02),
    }


def reference(x_prompt, x_sample, state_b_wkv, state_b_shift, norm_g, w_in, w_out, sgu_ln_g, sgu_ln_b,
              sgu_w, sgu_b, shift_mu, w0, w2, a0, a2, k_k, k_a, r_k, gn_g, gn_b, final_g):
    yp, ys = x_prompt, x_sample
    bp = x_prompt.shape[0]
    zero_wkv = jnp.zeros((bp, B_HEADS, B_HEAD_DIM, B_HEAD_DIM), jnp.float32)
    zero_shift = jnp.zeros((bp, 1, N_SHIFT), x_prompt.dtype)
    wkv_p, shift_p, wkv_s, shift_s, sgu_v_s = [], [], [], [], []
    for l in range(DEPTH):
        lw = [p[l] for p in (norm_g, w_in, w_out, sgu_ln_g, sgu_ln_b, sgu_w, sgu_b, shift_mu,
                             w0, w2, a0, a2, k_k, k_a, r_k, gn_g, gn_b)]
        yp, S_p, sh_p, _ = hybrid_layer(yp, zero_wkv, zero_shift, *lw)
        ys, S_s, sh_s, vn_s = hybrid_layer(ys, state_b_wkv[l], state_b_shift[l], *lw)
        wkv_p.append(S_p); shift_p.append(sh_p)
        wkv_s.append(S_s); shift_s.append(sh_s); sgu_v_s.append(vn_s)
    y_prompt = rms_norm(yp, final_g)
    y_sample = rms_norm(ys, final_g)
    new_wkv_prompt = jnp.stack(wkv_p)
    new_shift_prompt = jnp.stack(shift_p)
    new_wkv_sample = jnp.stack(wkv_s)
    new_shift_sample = jnp.stack(shift_s)
    new_sgu_v_sample = jnp.stack(sgu_v_s)
    return (y_prompt, y_sample, new_wkv_prompt, new_shift_prompt, new_wkv_sample, new_shift_sample, new_sgu_v_sample)
```

```python
import functools
import math

import jax
import jax.numpy as jnp
from jax import lax
from jax.experimental import pallas as pl
from jax.experimental.pallas import tpu as pltpu

D_MODEL = 2048
D_A = 1024
D_B = 1024
SGU_CHUNK = 128
A_HEADS = 8
A_HEAD_DIM = D_A // A_HEADS
B_HEAD_DIM = 64
B_HEADS = D_B // B_HEAD_DIM
LORA = 64
N_SHIFT = 3 * D_B + 2 * LORA
D_PROJ = 3 * D_A + N_SHIFT + D_B
NORM_EPS = 1e-6
LN_EPS = 1e-5
GN_EPS = 64e-5
EXP_NEG_W_OFFSET = math.exp(-0.5)

WKV_CHUNK = 64
HEADS_PER_GROUP = 4
GROUP_W = HEADS_PER_GROUP * B_HEAD_DIM
N_GROUPS = B_HEADS // HEADS_PER_GROUP

ROW_TILE = 256
IN_COL_TILE = 2432
VMEM_LIMIT = 56 * 1024 * 1024

F32 = jnp.float32
BF16 = jnp.bfloat16
HI = lax.Precision.HIGHEST


def _proj_in_kernel(x_ref, g_ref, w_ref, z_ref):
    x = x_ref[...]
    ms = jnp.mean(x * x, axis=-1, keepdims=True)
    h = x * lax.rsqrt(ms + NORM_EPS) * g_ref[...]
    z_ref[...] = jnp.dot(h.astype(BF16), w_ref[...], preferred_element_type=F32)


def _proj_in(x2d, norm_g, w_in_bf16):
    rows = x2d.shape[0]
    assert rows % ROW_TILE == 0 and D_PROJ % IN_COL_TILE == 0
    return pl.pallas_call(
        _proj_in_kernel,
        grid=(D_PROJ // IN_COL_TILE, rows // ROW_TILE),
        in_specs=[
            pl.BlockSpec((ROW_TILE, D_MODEL), lambda j, i: (i, 0)),
            pl.BlockSpec((1, D_MODEL), lambda j, i: (0, 0)),
            pl.BlockSpec((D_MODEL, IN_COL_TILE), lambda j, i: (0, j)),
        ],
        out_specs=pl.BlockSpec((ROW_TILE, IN_COL_TILE), lambda j, i: (i, j)),
        out_shape=jax.ShapeDtypeStruct((rows, D_PROJ), F32),
        compiler_params=pltpu.CompilerParams(
            dimension_semantics=("arbitrary", "arbitrary"), vmem_limit_bytes=VMEM_LIMIT),
        name="proj_in",
    )(x2d, norm_g.reshape(1, D_MODEL), w_in_bf16)


def _proj_out_kernel(x_ref, m_ref, w_ref, g_ref, y_ref, *, final_norm):
    o = jnp.dot(m_ref[...], w_ref[...], preferred_element_type=F32)
    y = x_ref[...] + o
    if final_norm:
        ms = jnp.mean(y * y, axis=-1, keepdims=True)
        y = y * lax.rsqrt(ms + NORM_EPS) * g_ref[...]
    y_ref[...] = y


def _proj_out(x2d, mixed2d, w_out_bf16, final_g, final_norm):
    rows = x2d.shape[0]
    assert rows % ROW_TILE == 0
    return pl.pallas_call(
        functools.partial(_proj_out_kernel, final_norm=final_norm),
        grid=(rows // ROW_TILE,),
        in_specs=[
            pl.BlockSpec((ROW_TILE, D_MODEL), lambda i: (i, 0)),
            pl.BlockSpec((ROW_TILE, D_MODEL), lambda i: (i, 0)),
            pl.BlockSpec((D_MODEL, D_MODEL), lambda i: (0, 0)),
            pl.BlockSpec((1, D_MODEL), lambda i: (0, 0)),
        ],
        out_specs=pl.BlockSpec((ROW_TILE, D_MODEL), lambda i: (i, 0)),
        out_shape=jax.ShapeDtypeStruct((rows, D_MODEL), F32),
        compiler_params=pltpu.CompilerParams(
            dimension_semantics=("arbitrary",), vmem_limit_bytes=VMEM_LIMIT),
        name="proj_out",
    )(x2d, mixed2d, w_out_bf16, final_g.reshape(1, D_MODEL))


def _gelu(x):
    return 0.5 * x * (1.0 + lax.erf(x * (1.0 / math.sqrt(2.0))))


def _silu(x):
    return x * jax.nn.sigmoid(x)


def _dot(a, b):
    return jnp.dot(a, b, precision=HI, preferred_element_type=F32)


def _dot_nt(a, b):
    return lax.dot_general(a, b, (((1,), (1,)), ((), ())), precision=HI, preferred_element_type=F32)


def _dot_tn(a, b):
    return lax.dot_general(a, b, (((0,), (0,)), ((), ())), precision=HI, preferred_element_type=F32)


def _iota(shape, dim):
    return lax.broadcasted_iota(jnp.int32, shape, dim)


def _mixer_kernel(z_ref, shift0_ref, wkv0_ref, mu_ref, lng_ref, lnb_ref, sguw_ref, sgub_ref,
                  wl_ref, bl_ref, kk_ref, ka_ref, rk_ref, gng_ref, gnb_ref,
                  mixed_ref, shift_out_ref, wkv_out_ref, *rest, tb, with_vn):
    if with_vn:
        vn_ref, s_scr, prev_scr = rest
    else:
        s_scr, prev_scr = rest
    c = WKV_CHUNK
    gw = GROUP_W
    t_idx = pl.program_id(1)
    n_t = pl.num_programs(1)

    bd_mask = _iota((gw, gw), 0) // B_HEAD_DIM == _iota((gw, gw), 1) // B_HEAD_DIM
    group_ones = bd_mask.astype(F32)

    @pl.when(t_idx == 0)
    def _init():
        prev_scr[...] = shift0_ref[0]
        tile_k = (_iota((B_HEAD_DIM, gw), 0) == _iota((B_HEAD_DIM, gw), 1) % B_HEAD_DIM).astype(F32)
        for g in range(N_GROUPS):
            s0 = wkv0_ref[0, g * gw:(g + 1) * gw, :]
            s_scr[g] = jnp.where(bd_mask, _dot(s0, tile_k), 0.0)

    def gsum(x):
        return jnp.concatenate(
            [_dot(x[:, g * gw:(g + 1) * gw], group_ones) for g in range(N_GROUPS)], axis=1)

    u = z_ref[0, :, 0:D_A]
    v = z_ref[0, :, D_A:2 * D_A]
    g_a = z_ref[0, :, 2 * D_A:3 * D_A]
    gv = _gelu(v)
    mean = jnp.mean(gv, axis=-1, keepdims=True)
    cen = gv - mean
    var = jnp.mean(cen * cen, axis=-1, keepdims=True)
    vn = cen * lax.rsqrt(var + LN_EPS) * lng_ref[...] + lnb_ref[...]
    if with_vn:
        vn_ref[0] = vn
    tril = _iota((tb, tb), 0) >= _iota((tb, tb), 1)
    vn_b = vn.astype(BF16)
    f_parts = []
    for h in range(A_HEADS):
        w = jnp.where(tril, sguw_ref[h], 0.0).astype(BF16)
        f_parts.append(jnp.dot(w, vn_b[:, h * A_HEAD_DIM:(h + 1) * A_HEAD_DIM],
                               preferred_element_type=F32))
    f = jnp.concatenate(f_parts, axis=1) + sgub_ref[...]
    out_a = _gelu(u) * f * _silu(g_a)

    zs = z_ref[0, :, 3 * D_A:3 * D_A + N_SHIFT]
    last_row = zs[tb - 1:tb, :]
    shifted = pltpu.roll(zs, 1, axis=0)
    z_prev = jnp.where(_iota((tb, N_SHIFT), 0) == 0, prev_scr[...], shifted)
    zmix = zs + mu_ref[...] * (z_prev - zs)
    prev_scr[...] = last_row
    shift_out_ref[0] = last_row

    r = zmix[:, 0:D_B]
    k = zmix[:, D_B:2 * D_B]
    vb = zmix[:, 2 * D_B:3 * D_B]
    lo = zmix[:, 3 * D_B:3 * D_B + 2 * LORA]
    lo_in = jnp.where(_iota((tb, 2 * LORA), 1) < LORA, jnp.tanh(lo), lo)
    da = _dot(lo_in, wl_ref[...]) + bl_ref[...]
    logw = -EXP_NEG_W_OFFSET * jax.nn.sigmoid(da[:, 0:D_B])
    a = jax.nn.sigmoid(da[:, D_B:2 * D_B])

    kk = k * kk_ref[...]
    kk = kk / jnp.maximum(jnp.sqrt(gsum(kk * kk)), 1e-12)
    k = k * (1.0 + (a - 1.0) * ka_ref[...])
    b = kk * a

    same_chunk = _iota((tb, tb), 0) // c == _iota((tb, tb), 1) // c
    cum = _dot(jnp.where(tril & same_chunk, 1.0, 0.0), logw)

    t_w = _iota((c, gw), 0)
    s_w = _iota((c, gw), 1) % c
    strict_w = t_w > s_w
    incl_w = t_w >= s_w
    eye_w = (t_w == s_w).astype(F32)
    i_bd = _iota((gw, gw), 0) % c
    j_bd = _iota((gw, gw), 1) % c
    head_w = _iota((c, gw), 1) // B_HEAD_DIM

    def block_diag(x):
        return jnp.where(bd_mask, jnp.concatenate([x] * HEADS_PER_GROUP, axis=0), 0.0)

    y_rows = []
    for ci in range(tb // c):
        rows = slice(ci * c, (ci + 1) * c)
        cum_c = cum[rows]
        logw_c = logw[rows]
        cum_last = cum_c[c - 1:c, :]
        g_in = jnp.exp(cum_c)
        g_prev = jnp.exp(cum_c - logw_c)
        g_inv = jnp.exp(-cum_c)
        g_hat = jnp.exp(cum_last - cum_c)
        g_last = jnp.exp(cum_last)
        r_t = r[rows] * g_in
        kk_t = kk[rows] * g_prev
        b_t = b[rows] * g_inv
        k_t = k[rows] * g_inv
        b_h = b[rows] * g_hat
        k_h = k[rows] * g_hat
        v_c = vb[rows]
        y_groups = []
        for g in range(N_GROUPS):
            lanes = slice(g * gw, (g + 1) * gw)
            s_old = s_scr[g]
            lhs = jnp.concatenate([kk_t[:, lanes], r_t[:, lanes]], axis=0)
            ph = _dot_nt(lhs, s_old)
            rb = jnp.concatenate(
                [jnp.where(head_w == h, b_t[:, lanes], 0.0) for h in range(HEADS_PER_GROUP)]
                + [jnp.where(head_w == h, k_t[:, lanes], 0.0) for h in range(HEADS_PER_GROUP)],
                axis=0)
            a_all = _dot_nt(lhs, rb)
            a_b, a_k = a_all[:c, :gw], a_all[:c, gw:]
            a_rb, a_rk = a_all[c:, :gw], a_all[c:, gw:]
            l_w = jnp.where(strict_w, a_b, 0.0)
            v_g = v_c[:, lanes]
            v_bd = block_diag(v_g)
            rhs = ph[:c] + _dot(jnp.where(strict_w, a_k, 0.0), v_bd)
            l_bd = block_diag(l_w)
            d_w = eye_w - jnp.where((t_w % 2 == 1) & (s_w == t_w - 1), l_w, 0.0)
            m = 2
            while m < c:
                off_mask = bd_mask & (i_bd // (2 * m) == j_bd // (2 * m)) \
                    & ((i_bd // m) % 2 == 1) & ((j_bd // m) % 2 == 0)
                off_bd = jnp.where(off_mask, l_bd, 0.0)
                d_w = d_w - _dot(_dot(d_w, off_bd), block_diag(d_w))
                m *= 2
            u_g = -_dot(d_w, block_diag(rhs))
            y_g = (ph[c:] + _dot(jnp.where(incl_w, a_rb, 0.0), block_diag(u_g))
                   + _dot(jnp.where(incl_w, a_rk, 0.0), v_bd))
            upd = _dot_tn(jnp.concatenate([u_g, v_g], axis=0),
                          jnp.concatenate([b_h[:, lanes], k_h[:, lanes]], axis=0))
            s_scr[g] = s_old * g_last[:, lanes] + jnp.where(bd_mask, upd, 0.0)
            y_groups.append(y_g)
        y_rows.append(jnp.concatenate(y_groups, axis=1))
    y = jnp.concatenate(y_rows, axis=0) if len(y_rows) > 1 else y_rows[0]

    mu_y = gsum(y) * (1.0 / B_HEAD_DIM)
    y_cen = y - mu_y
    var_y = gsum(y_cen * y_cen) * (1.0 / B_HEAD_DIM)
    y = y_cen * lax.rsqrt(var_y + GN_EPS) * gng_ref[...] + gnb_ref[...]
    y = y + gsum(r * k * rk_ref[...]) * vb
    g_b = z_ref[0, :, 3 * D_A + N_SHIFT:D_PROJ]
    out_b = y * _silu(g_b)

    mixed_ref[0, :, 0:D_A] = out_a.astype(mixed_ref.dtype)
    mixed_ref[0, :, D_A:D_A + D_B] = out_b.astype(mixed_ref.dtype)

    @pl.when(t_idx == n_t - 1)
    def _final():
        untile_k = (_iota((gw, B_HEAD_DIM), 0) % B_HEAD_DIM == _iota((gw, B_HEAD_DIM), 1)).astype(F32)
        for g in range(N_GROUPS):
            wkv_out_ref[0, g * gw:(g + 1) * gw, :] = _dot(s_scr[g], untile_k)


def _mixer(z, shift0, wkv0, lw, with_vn):
    bsz, t, _ = z.shape
    tb = min(t, SGU_CHUNK)
    assert t % tb == 0 and tb % WKV_CHUNK == 0
    n_t = t // tb
    row = lambda p: p.reshape(1, -1)
    sgu_w = lw["sgu_w"][:, :tb, :tb]
    sgu_b = jnp.repeat(lw["sgu_b"][:, :tb].T, A_HEAD_DIM, axis=1)
    zeros = jnp.zeros((LORA, D_B), F32)
    w_lora = jnp.concatenate([jnp.concatenate([lw["w2"], zeros], axis=1),
                              jnp.concatenate([zeros, lw["a2"]], axis=1)], axis=0)
    b_lora = jnp.concatenate([lw["w0"], lw["a0"]]).reshape(1, 2 * D_B)
    const = lambda shape: pl.BlockSpec(shape, lambda b_, t_: (0,) * len(shape))
    in_specs = [
        pl.BlockSpec((1, tb, D_PROJ), lambda b_, t_: (b_, t_, 0)),
        pl.BlockSpec((1, 1, N_SHIFT), lambda b_, t_: (b_, 0, 0)),
        pl.BlockSpec((1, D_B, B_HEAD_DIM), lambda b_, t_: (b_, 0, 0)),
        const((1, N_SHIFT)), const((1, D_A)), const((1, D_A)),
        const((A_HEADS, tb, tb)), const((tb, D_A)),
        const((2 * LORA, 2 * D_B)), const((1, 2 * D_B)),
        const((1, D_B)), const((1, D_B)), const((1, D_B)), const((1, D_B)), const((1, D_B)),
    ]
    out_specs = [
        pl.BlockSpec((1, tb, D_A + D_B), lambda b_, t_: (b_, t_, 0)),
        pl.BlockSpec((1, 1, N_SHIFT), lambda b_, t_: (b_, 0, 0)),
        pl.BlockSpec((1, D_B, B_HEAD_DIM), lambda b_, t_: (b_, 0, 0)),
    ]
    out_shape = [
        jax.ShapeDtypeStruct((bsz, t, D_A + D_B), BF16),
        jax.ShapeDtypeStruct((bsz, 1, N_SHIFT), F32),
        jax.ShapeDtypeStruct((bsz, D_B, B_HEAD_DIM), F32),
    ]
    if with_vn:
        out_specs.append(pl.BlockSpec((1, tb, D_A), lambda b_, t_: (b_, t_, 0)))
        out_shape.append(jax.ShapeDtypeStruct((bsz, t, D_A), F32))
    outs = pl.pallas_call(
        functools.partial(_mixer_kernel, tb=tb, with_vn=with_vn),
        grid=(bsz, n_t),
        in_specs=in_specs,
        out_specs=out_specs,
        out_shape=out_shape,
        scratch_shapes=[pltpu.VMEM((N_GROUPS, GROUP_W, GROUP_W), F32),
                        pltpu.VMEM((1, N_SHIFT), F32)],
        compiler_params=pltpu.CompilerParams(
            dimension_semantics=("arbitrary", "arbitrary"), vmem_limit_bytes=VMEM_LIMIT),
        name="mixer",
    )(z, shift0, wkv0.reshape(bsz, D_B, B_HEAD_DIM),
      row(lw["shift_mu"]), row(lw["sgu_ln_g"]), row(lw["sgu_ln_b"]), sgu_w, sgu_b,
      w_lora, b_lora, row(lw["k_k"]), row(lw["k_a"]), row(lw["r_k"]), row(lw["gn_g"]), row(lw["gn_b"]))
    mixed, new_shift, wkv = outs[:3]
    vn = outs[3] if with_vn else None
    return mixed, new_shift, wkv.reshape(bsz, B_HEADS, B_HEAD_DIM, B_HEAD_DIM), vn


def _hybrid_layer(x, wkv0, shift0, lw, final_g, final_norm, with_vn):
    bsz, t, _ = x.shape
    x2d = x.reshape(bsz * t, D_MODEL)
    z = _proj_in(x2d, lw["norm_g"], lw["w_in"]).reshape(bsz, t, D_PROJ)
    mixed, new_shift, wkv, vn = _mixer(z, shift0, wkv0, lw, with_vn)
    y = _proj_out(x2d, mixed.reshape(bsz * t, D_MODEL), lw["w_out"], final_g, final_norm)
    return y.reshape(bsz, t, D_MODEL), wkv, new_shift, vn


_LAYER_PARAMS = ("norm_g", "w_in", "w_out", "sgu_ln_g", "sgu_ln_b", "sgu_w", "sgu_b", "shift_mu",
                 "w0", "w2", "a0", "a2", "k_k", "k_a", "r_k", "gn_g", "gn_b")


def kernel(x_prompt, x_sample, state_b_wkv, state_b_shift, norm_g, w_in, w_out, sgu_ln_g, sgu_ln_b,
           sgu_w, sgu_b, shift_mu, w0, w2, a0, a2, k_k, k_a, r_k, gn_g, gn_b, final_g):
    stacked = dict(zip(_LAYER_PARAMS, (norm_g, w_in, w_out, sgu_ln_g, sgu_ln_b, sgu_w, sgu_b, shift_mu,
                                       w0, w2, a0, a2, k_k, k_a, r_k, gn_g, gn_b)))
    depth = w_in.shape[0]
    bp = x_prompt.shape[0]
    zero_wkv = jnp.zeros((bp, B_HEADS, B_HEAD_DIM, B_HEAD_DIM), F32)
    zero_shift = jnp.zeros((bp, 1, N_SHIFT), F32)
    yp, ys = x_prompt, x_sample
    wkv_p, shift_p, wkv_s, shift_s, sgu_v_s = [], [], [], [], []
    for l in range(depth):
        lw = {name: p[l] for name, p in stacked.items()}
        lw["w_in"] = lw["w_in"].astype(BF16)
        lw["w_out"] = lw["w_out"].astype(BF16)
        last = l == depth - 1
        yp, s_p, sh_p, _ = _hybrid_layer(yp, zero_wkv, zero_shift, lw, final_g, last, False)
        ys, s_s, sh_s, vn_s = _hybrid_layer(ys, state_b_wkv[l], state_b_shift[l], lw, final_g, last, True)
        wkv_p.append(s_p); shift_p.append(sh_p)
        wkv_s.append(s_s); shift_s.append(sh_s); sgu_v_s.append(vn_s)
    return (yp, ys, jnp.stack(wkv_p), jnp.stack(shift_p), jnp.stack(wkv_s), jnp.stack(shift_s),
            jnp.stack(sgu_v_s))
```

```python
import functools
import math

import jax
import jax.numpy as jnp
from jax import lax
from jax.experimental import pallas as pl
from jax.experimental.pallas import tpu as pltpu

D_MODEL = 2048
D_A = 1024
D_B = 1024
SGU_CHUNK = 128
A_HEADS = 8
A_HEAD_DIM = D_A // A_HEADS
B_HEAD_DIM = 64
B_HEADS = D_B // B_HEAD_DIM
LORA = 64
N_SHIFT = 3 * D_B + 2 * LORA
D_PROJ = 3 * D_A + N_SHIFT + D_B
NORM_EPS = 1e-6
LN_EPS = 1e-5
GN_EPS = 64e-5
EXP_NEG_W_OFFSET = math.exp(-0.5)

WKV_CHUNK = 64
HEADS_PER_GROUP = 4
GROUP_W = HEADS_PER_GROUP * B_HEAD_DIM
N_GROUPS = B_HEADS // HEADS_PER_GROUP

ROW_TILE = 256
IN_COL_TILE = 2432
VMEM_LIMIT = 56 * 1024 * 1024

F32 = jnp.float32
BF16 = jnp.bfloat16


def _proj_in_kernel(x_ref, g_ref, w_ref, z_ref):
    x = x_ref[...]
    ms = jnp.mean(x * x, axis=-1, keepdims=True)
    h = x * lax.rsqrt(ms + NORM_EPS) * g_ref[...]
    z_ref[...] = jnp.dot(h.astype(BF16), w_ref[...], preferred_element_type=F32)


def _proj_in(x2d, norm_g, w_in_bf16):
    rows = x2d.shape[0]
    assert rows % ROW_TILE == 0 and D_PROJ % IN_COL_TILE == 0
    return pl.pallas_call(
        _proj_in_kernel,
        grid=(D_PROJ // IN_COL_TILE, rows // ROW_TILE),
        in_specs=[
            pl.BlockSpec((ROW_TILE, D_MODEL), lambda j, i: (i, 0)),
            pl.BlockSpec((1, D_MODEL), lambda j, i: (0, 0)),
            pl.BlockSpec((D_MODEL, IN_COL_TILE), lambda j, i: (0, j)),
        ],
        out_specs=pl.BlockSpec((ROW_TILE, IN_COL_TILE), lambda j, i: (i, j)),
        out_shape=jax.ShapeDtypeStruct((rows, D_PROJ), F32),
        compiler_params=pltpu.CompilerParams(
            dimension_semantics=("arbitrary", "arbitrary"), vmem_limit_bytes=VMEM_LIMIT),
        name="proj_in",
    )(x2d, norm_g.reshape(1, D_MODEL), w_in_bf16)


def _proj_out_kernel(x_ref, m_ref, w_ref, g_ref, y_ref, *, final_norm):
    o = jnp.dot(m_ref[...], w_ref[...], preferred_element_type=F32)
    y = x_ref[...] + o
    if final_norm:
        ms = jnp.mean(y * y, axis=-1, keepdims=True)
        y = y * lax.rsqrt(ms + NORM_EPS) * g_ref[...]
    y_ref[...] = y


def _proj_out(x2d, mixed2d, w_out_bf16, final_g, final_norm):
    rows = x2d.shape[0]
    assert rows % ROW_TILE == 0
    return pl.pallas_call(
        functools.partial(_proj_out_kernel, final_norm=final_norm),
        grid=(rows // ROW_TILE,),
        in_specs=[
            pl.BlockSpec((ROW_TILE, D_MODEL), lambda i: (i, 0)),
            pl.BlockSpec((ROW_TILE, D_MODEL), lambda i: (i, 0)),
            pl.BlockSpec((D_MODEL, D_MODEL), lambda i: (0, 0)),
            pl.BlockSpec((1, D_MODEL), lambda i: (0, 0)),
        ],
        out_specs=pl.BlockSpec((ROW_TILE, D_MODEL), lambda i: (i, 0)),
        out_shape=jax.ShapeDtypeStruct((rows, D_MODEL), F32),
        compiler_params=pltpu.CompilerParams(
            dimension_semantics=("arbitrary",), vmem_limit_bytes=VMEM_LIMIT),
        name="proj_out",
    )(x2d, mixed2d, w_out_bf16, final_g.reshape(1, D_MODEL))


def _gelu(x):
    return 0.5 * x * (1.0 + lax.erf(x * (1.0 / math.sqrt(2.0))))


def _silu(x):
    return x * jax.nn.sigmoid(x)


_NN = (((1,), (0,)), ((), ()))
_NT = (((1,), (1,)), ((), ()))
_TN = (((0,), (0,)), ((), ()))


def _split_bf16(x, n):
    parts = []
    for _ in range(n - 1):
        p = x.astype(BF16)
        parts.append(p)
        x = x - p.astype(F32)
    parts.append(x.astype(BF16))
    return parts


def _dg(a, b, dims):
    return lax.dot_general(a, b, dims, preferred_element_type=F32)


def _mm(a, b, dims=_NN):
    return _dg(a.astype(BF16), b.astype(BF16), dims)


def _mm_3pass(a, b, dims=_NN):
    ah, al = _split_bf16(a, 2)
    bh, bl = _split_bf16(b, 2)
    return _dg(ah, bh, dims) + _dg(ah, bl, dims) + _dg(al, bh, dims)


def _mm_exact_rhs(a, b, n):
    bb = b.astype(BF16)
    return sum(_dg(p, bb, _NN) for p in _split_bf16(a, n))


def _mm_exact_lhs(a, b, n):
    ab = a.astype(BF16)
    return sum(_dg(ab, p, _NN) for p in _split_bf16(b, n))


def _iota(shape, dim):
    return lax.broadcasted_iota(jnp.int32, shape, dim)


def _mixer_kernel(z_ref, shift0_ref, wkv0_ref, mu_ref, lng_ref, lnb_ref, sguw_ref, sgub_ref,
                  wl_ref, bl_ref, kk_ref, ka_ref, rk_ref, gng_ref, gnb_ref,
                  mixed_ref, shift_out_ref, wkv_out_ref, *rest, tb, with_vn):
    if with_vn:
        vn_ref, s_scr, prev_scr = rest
    else:
        s_scr, prev_scr = rest
    c = WKV_CHUNK
    gw = GROUP_W
    t_idx = pl.program_id(1)
    n_t = pl.num_programs(1)

    bd_mask = _iota((gw, gw), 0) // B_HEAD_DIM == _iota((gw, gw), 1) // B_HEAD_DIM
    group_ones = bd_mask.astype(F32)

    @pl.when(t_idx == 0)
    def _init():
        prev_scr[...] = shift0_ref[0]
        tile_k = (_iota((B_HEAD_DIM, gw), 0) == _iota((B_HEAD_DIM, gw), 1) % B_HEAD_DIM).astype(F32)
        for g in range(N_GROUPS):
            s0 = wkv0_ref[0, g * gw:(g + 1) * gw, :]
            s_scr[g] = jnp.where(bd_mask, _mm_exact_rhs(s0, tile_k, 3), 0.0)

    def gsum(x):
        return jnp.concatenate(
            [_mm_exact_rhs(x[:, g * gw:(g + 1) * gw], group_ones, 2) for g in range(N_GROUPS)], axis=1)

    u = z_ref[0, :, 0:D_A]
    v = z_ref[0, :, D_A:2 * D_A]
    g_a = z_ref[0, :, 2 * D_A:3 * D_A]
    gv = _gelu(v)
    mean = jnp.mean(gv, axis=-1, keepdims=True)
    cen = gv - mean
    var = jnp.mean(cen * cen, axis=-1, keepdims=True)
    vn = cen * lax.rsqrt(var + LN_EPS) * lng_ref[...] + lnb_ref[...]
    if with_vn:
        vn_ref[0] = vn
    tril = _iota((tb, tb), 0) >= _iota((tb, tb), 1)
    vn_b = vn.astype(BF16)
    f_parts = []
    for h in range(A_HEADS):
        w = jnp.where(tril, sguw_ref[h], 0.0).astype(BF16)
        f_parts.append(jnp.dot(w, vn_b[:, h * A_HEAD_DIM:(h + 1) * A_HEAD_DIM],
                               preferred_element_type=F32))
    f = jnp.concatenate(f_parts, axis=1) + sgub_ref[...]
    out_a = _gelu(u) * f * _silu(g_a)

    zs = z_ref[0, :, 3 * D_A:3 * D_A + N_SHIFT]
    last_row = zs[tb - 1:tb, :]
    shifted = pltpu.roll(zs, 1, axis=0)
    z_prev = jnp.where(_iota((tb, N_SHIFT), 0) == 0, prev_scr[...], shifted)
    zmix = zs + mu_ref[...] * (z_prev - zs)
    prev_scr[...] = last_row
    shift_out_ref[0] = last_row

    r = zmix[:, 0:D_B]
    k = zmix[:, D_B:2 * D_B]
    vb = zmix[:, 2 * D_B:3 * D_B]
    lo = zmix[:, 3 * D_B:3 * D_B + 2 * LORA]
    lo_in = jnp.where(_iota((tb, 2 * LORA), 1) < LORA, jnp.tanh(lo), lo)
    da = _mm(lo_in, wl_ref[...]) + bl_ref[...]
    logw = -EXP_NEG_W_OFFSET * jax.nn.sigmoid(da[:, 0:D_B])
    a = jax.nn.sigmoid(da[:, D_B:2 * D_B])

    kk = k * kk_ref[...]
    kk = kk / jnp.maximum(jnp.sqrt(gsum(kk * kk)), 1e-12)
    k = k * (1.0 + (a - 1.0) * ka_ref[...])
    b = kk * a

    same_chunk = _iota((tb, tb), 0) // c == _iota((tb, tb), 1) // c
    cum = _mm_exact_lhs(jnp.where(tril & same_chunk, 1.0, 0.0), logw, 3)

    t_w = _iota((c, gw), 0)
    s_w = _iota((c, gw), 1) % c
    strict_w = t_w > s_w
    incl_w = t_w >= s_w
    eye_w = (t_w == s_w).astype(F32)
    i_bd = _iota((gw, gw), 0) % c
    j_bd = _iota((gw, gw), 1) % c
    head_w = _iota((c, gw), 1) // B_HEAD_DIM

    def block_diag(x):
        return jnp.where(bd_mask, jnp.concatenate([x] * HEADS_PER_GROUP, axis=0), 0.0)

    y_rows = []
    for ci in range(tb // c):
        rows = slice(ci * c, (ci + 1) * c)
        cum_c = cum[rows]
        logw_c = logw[rows]
        cum_last = cum_c[c - 1:c, :]
        g_in = jnp.exp(cum_c)
        g_prev = jnp.exp(cum_c - logw_c)
        g_inv = jnp.exp(-cum_c)
        g_hat = jnp.exp(cum_last - cum_c)
        g_last = jnp.exp(cum_last)
        r_t = r[rows] * g_in
        kk_t = kk[rows] * g_prev
        b_t = b[rows] * g_inv
        k_t = k[rows] * g_inv
        b_h = b[rows] * g_hat
        k_h = k[rows] * g_hat
        v_c = vb[rows]
        y_groups = []
        for g in range(N_GROUPS):
            lanes = slice(g * gw, (g + 1) * gw)
            s_old = s_scr[g]
            lhs = jnp.concatenate([kk_t[:, lanes], r_t[:, lanes]], axis=0)
            ph = _mm(lhs, s_old, _NT)
            rb = jnp.concatenate(
                [jnp.where(head_w == h, b_t[:, lanes], 0.0) for h in range(HEADS_PER_GROUP)]
                + [jnp.where(head_w == h, k_t[:, lanes], 0.0) for h in range(HEADS_PER_GROUP)],
                axis=0)
            a_all = _mm(lhs, rb, _NT)
            a_b, a_k = a_all[:c, :gw], a_all[:c, gw:]
            a_rb, a_rk = a_all[c:, :gw], a_all[c:, gw:]
            l_w = jnp.where(strict_w, a_b, 0.0)
            v_g = v_c[:, lanes]
            v_bd = block_diag(v_g)
            rhs = ph[:c] + _mm(jnp.where(strict_w, a_k, 0.0), v_bd)
            l_bd = block_diag(l_w)
            d_w = eye_w - jnp.where((t_w % 2 == 1) & (s_w == t_w - 1), l_w, 0.0)
            m = 2
            while m < c:
                off_mask = bd_mask & (i_bd // (2 * m) == j_bd // (2 * m)) \
                    & ((i_bd // m) % 2 == 1) & ((j_bd // m) % 2 == 0)
                off_bd = jnp.where(off_mask, l_bd, 0.0)
                d_w = d_w - _mm(_mm(d_w, off_bd), block_diag(d_w))
                m *= 2
            u_g = -_mm(d_w, block_diag(rhs))
            y_g = (ph[c:] + _mm(jnp.where(incl_w, a_rb, 0.0), block_diag(u_g))
                   + _mm(jnp.where(incl_w, a_rk, 0.0), v_bd))
            upd = _mm_3pass(jnp.concatenate([u_g, v_g], axis=0),
                            jnp.concatenate([b_h[:, lanes], k_h[:, lanes]], axis=0), _TN)
            s_scr[g] = s_old * g_last[:, lanes] + jnp.where(bd_mask, upd, 0.0)
            y_groups.append(y_g)
        y_rows.append(jnp.concatenate(y_groups, axis=1))
    y = jnp.concatenate(y_rows, axis=0) if len(y_rows) > 1 else y_rows[0]

    mu_y = gsum(y) * (1.0 / B_HEAD_DIM)
    y_cen = y - mu_y
    var_y = gsum(y_cen * y_cen) * (1.0 / B_HEAD_DIM)
    y = y_cen * lax.rsqrt(var_y + GN_EPS) * gng_ref[...] + gnb_ref[...]
    y = y + gsum(r * k * rk_ref[...]) * vb
    g_b = z_ref[0, :, 3 * D_A + N_SHIFT:D_PROJ]
    out_b = y * _silu(g_b)

    mixed_ref[0, :, 0:D_A] = out_a.astype(mixed_ref.dtype)
    mixed_ref[0, :, D_A:D_A + D_B] = out_b.astype(mixed_ref.dtype)

    @pl.when(t_idx == n_t - 1)
    def _final():
        untile_k = (_iota((gw, B_HEAD_DIM), 0) % B_HEAD_DIM == _iota((gw, B_HEAD_DIM), 1)).astype(F32)
        for g in range(N_GROUPS):
            wkv_out_ref[0, g * gw:(g + 1) * gw, :] = _mm_exact_rhs(s_scr[g], untile_k, 3)


def _mixer(z, shift0, wkv0, lw, with_vn):
    bsz, t, _ = z.shape
    tb = min(t, SGU_CHUNK)
    assert t % tb == 0 and tb % WKV_CHUNK == 0
    n_t = t // tb
    row = lambda p: p.reshape(1, -1)
    sgu_w = lw["sgu_w"][:, :tb, :tb]
    sgu_b = jnp.repeat(lw["sgu_b"][:, :tb].T, A_HEAD_DIM, axis=1)
    zeros = jnp.zeros((LORA, D_B), F32)
    w_lora = jnp.concatenate([jnp.concatenate([lw["w2"], zeros], axis=1),
                              jnp.concatenate([zeros, lw["a2"]], axis=1)], axis=0)
    b_lora = jnp.concatenate([lw["w0"], lw["a0"]]).reshape(1, 2 * D_B)
    const = lambda shape: pl.BlockSpec(shape, lambda b_, t_: (0,) * len(shape))
    in_specs = [
        pl.BlockSpec((1, tb, D_PROJ), lambda b_, t_: (b_, t_, 0)),
        pl.BlockSpec((1, 1, N_SHIFT), lambda b_, t_: (b_, 0, 0)),
        pl.BlockSpec((1, D_B, B_HEAD_DIM), lambda b_, t_: (b_, 0, 0)),
        const((1, N_SHIFT)), const((1, D_A)), const((1, D_A)),
        const((A_HEADS, tb, tb)), const((tb, D_A)),
        const((2 * LORA, 2 * D_B)), const((1, 2 * D_B)),
        const((1, D_B)), const((1, D_B)), const((1, D_B)), const((1, D_B)), const((1, D_B)),
    ]
    out_specs = [
        pl.BlockSpec((1, tb, D_A + D_B), lambda b_, t_: (b_, t_, 0)),
        pl.BlockSpec((1, 1, N_SHIFT), lambda b_, t_: (b_, 0, 0)),
        pl.BlockSpec((1, D_B, B_HEAD_DIM), lambda b_, t_: (b_, 0, 0)),
    ]
    out_shape = [
        jax.ShapeDtypeStruct((bsz, t, D_A + D_B), BF16),
        jax.ShapeDtypeStruct((bsz, 1, N_SHIFT), F32),
        jax.ShapeDtypeStruct((bsz, D_B, B_HEAD_DIM), F32),
    ]
    if with_vn:
        out_specs.append(pl.BlockSpec((1, tb, D_A), lambda b_, t_: (b_, t_, 0)))
        out_shape.append(jax.ShapeDtypeStruct((bsz, t, D_A), F32))
    outs = pl.pallas_call(
        functools.partial(_mixer_kernel, tb=tb, with_vn=with_vn),
        grid=(bsz, n_t),
        in_specs=in_specs,
        out_specs=out_specs,
        out_shape=out_shape,
        scratch_shapes=[pltpu.VMEM((N_GROUPS, GROUP_W, GROUP_W), F32),
                        pltpu.VMEM((1, N_SHIFT), F32)],
        compiler_params=pltpu.CompilerParams(
            dimension_semantics=("arbitrary", "arbitrary"), vmem_limit_bytes=VMEM_LIMIT),
        name="mixer",
    )(z, shift0, wkv0.reshape(bsz, D_B, B_HEAD_DIM),
      row(lw["shift_mu"]), row(lw["sgu_ln_g"]), row(lw["sgu_ln_b"]), sgu_w, sgu_b,
      w_lora, b_lora, row(lw["k_k"]), row(lw["k_a"]), row(lw["r_k"]), row(lw["gn_g"]), row(lw["gn_b"]))
    mixed, new_shift, wkv = outs[:3]
    vn = outs[3] if with_vn else None
    return mixed, new_shift, wkv.reshape(bsz, B_HEADS, B_HEAD_DIM, B_HEAD_DIM), vn


def _hybrid_layer(x, wkv0, shift0, lw, final_g, final_norm, with_vn):
    bsz, t, _ = x.shape
    x2d = x.reshape(bsz * t, D_MODEL)
    z = _proj_in(x2d, lw["norm_g"], lw["w_in"]).reshape(bsz, t, D_PROJ)
    mixed, new_shift, wkv, vn = _mixer(z, shift0, wkv0, lw, with_vn)
    y = _proj_out(x2d, mixed.reshape(bsz * t, D_MODEL), lw["w_out"], final_g, final_norm)
    return y.reshape(bsz, t, D_MODEL), wkv, new_shift, vn


_LAYER_PARAMS = ("norm_g", "w_in", "w_out", "sgu_ln_g", "sgu_ln_b", "sgu_w", "sgu_b", "shift_mu",
                 "w0", "w2", "a0", "a2", "k_k", "k_a", "r_k", "gn_g", "gn_b")


def kernel(x_prompt, x_sample, state_b_wkv, state_b_shift, norm_g, w_in, w_out, sgu_ln_g, sgu_ln_b,
           sgu_w, sgu_b, shift_mu, w0, w2, a0, a2, k_k, k_a, r_k, gn_g, gn_b, final_g):
    stacked = dict(zip(_LAYER_PARAMS, (norm_g, w_in, w_out, sgu_ln_g, sgu_ln_b, sgu_w, sgu_b, shift_mu,
                                       w0, w2, a0, a2, k_k, k_a, r_k, gn_g, gn_b)))
    depth = w_in.shape[0]
    bp = x_prompt.shape[0]
    zero_wkv = jnp.zeros((bp, B_HEADS, B_HEAD_DIM, B_HEAD_DIM), F32)
    zero_shift = jnp.zeros((bp, 1, N_SHIFT), F32)
    yp, ys = x_prompt, x_sample
    wkv_p, shift_p, wkv_s, shift_s, sgu_v_s = [], [], [], [], []
    for l in range(depth):
        lw = {name: p[l] for name, p in stacked.items()}
        lw["w_in"] = lw["w_in"].astype(BF16)
        lw["w_out"] = lw["w_out"].astype(BF16)
        last = l == depth - 1
        yp, s_p, sh_p, _ = _hybrid_layer(yp, zero_wkv, zero_shift, lw, final_g, last, False)
        ys, s_s, sh_s, vn_s = _hybrid_layer(ys, state_b_wkv[l], state_b_shift[l], lw, final_g, last, True)
        wkv_p.append(s_p); shift_p.append(sh_p)
        wkv_s.append(s_s); shift_s.append(sh_s); sgu_v_s.append(vn_s)
    return (yp, ys, jnp.stack(wkv_p), jnp.stack(shift_p), jnp.stack(wkv_s), jnp.stack(shift_s),
            jnp.stack(sgu_v_s))
```

```python
import functools
import math

import jax
import jax.numpy as jnp
from jax import lax
from jax.experimental import pallas as pl
from jax.experimental.pallas import tpu as pltpu

D_MODEL = 2048
D_A = 1024
D_B = 1024
SGU_CHUNK = 128
A_HEADS = 8
A_HEAD_DIM = D_A // A_HEADS
B_HEAD_DIM = 64
B_HEADS = D_B // B_HEAD_DIM
LORA = 64
N_SHIFT = 3 * D_B + 2 * LORA
D_PROJ = 3 * D_A + N_SHIFT + D_B
NORM_EPS = 1e-6
LN_EPS = 1e-5
GN_EPS = 64e-5
EXP_NEG_W_OFFSET = math.exp(-0.5)

WKV_CHUNK = 64
HEADS_PER_GROUP = 4
GROUP_W = HEADS_PER_GROUP * B_HEAD_DIM
N_GROUPS = B_HEADS // HEADS_PER_GROUP

ROW_TILE = 256
TIME_BLOCK = 256
IN_COL_TILE = 2432
VMEM_LIMIT = 56 * 1024 * 1024

F32 = jnp.float32
BF16 = jnp.bfloat16


def _proj_in_kernel(x_ref, g_ref, w_ref, z_ref):
    x = x_ref[...]
    ms = jnp.mean(x * x, axis=-1, keepdims=True)
    h = x * lax.rsqrt(ms + NORM_EPS) * g_ref[...]
    z_ref[...] = jnp.dot(h.astype(BF16), w_ref[...], preferred_element_type=F32)


def _proj_in(x2d, norm_g, w_in_bf16):
    rows = x2d.shape[0]
    assert rows % ROW_TILE == 0 and D_PROJ % IN_COL_TILE == 0
    return pl.pallas_call(
        _proj_in_kernel,
        grid=(D_PROJ // IN_COL_TILE, rows // ROW_TILE),
        in_specs=[
            pl.BlockSpec((ROW_TILE, D_MODEL), lambda j, i: (i, 0)),
            pl.BlockSpec((1, D_MODEL), lambda j, i: (0, 0)),
            pl.BlockSpec((D_MODEL, IN_COL_TILE), lambda j, i: (0, j)),
        ],
        out_specs=pl.BlockSpec((ROW_TILE, IN_COL_TILE), lambda j, i: (i, j)),
        out_shape=jax.ShapeDtypeStruct((rows, D_PROJ), F32),
        compiler_params=pltpu.CompilerParams(
            dimension_semantics=("arbitrary", "arbitrary"), vmem_limit_bytes=VMEM_LIMIT),
        name="proj_in",
    )(x2d, norm_g.reshape(1, D_MODEL), w_in_bf16)


def _proj_out_kernel(x_ref, m_ref, w_ref, g_ref, y_ref, *, final_norm):
    o = jnp.dot(m_ref[...], w_ref[...], preferred_element_type=F32)
    y = x_ref[...] + o
    if final_norm:
        ms = jnp.mean(y * y, axis=-1, keepdims=True)
        y = y * lax.rsqrt(ms + NORM_EPS) * g_ref[...]
    y_ref[...] = y


def _proj_out(x2d, mixed2d, w_out_bf16, final_g, final_norm):
    rows = x2d.shape[0]
    assert rows % ROW_TILE == 0
    return pl.pallas_call(
        functools.partial(_proj_out_kernel, final_norm=final_norm),
        grid=(rows // ROW_TILE,),
        in_specs=[
            pl.BlockSpec((ROW_TILE, D_MODEL), lambda i: (i, 0)),
            pl.BlockSpec((ROW_TILE, D_MODEL), lambda i: (i, 0)),
            pl.BlockSpec((D_MODEL, D_MODEL), lambda i: (0, 0)),
            pl.BlockSpec((1, D_MODEL), lambda i: (0, 0)),
        ],
        out_specs=pl.BlockSpec((ROW_TILE, D_MODEL), lambda i: (i, 0)),
        out_shape=jax.ShapeDtypeStruct((rows, D_MODEL), F32),
        compiler_params=pltpu.CompilerParams(
            dimension_semantics=("arbitrary",), vmem_limit_bytes=VMEM_LIMIT),
        name="proj_out",
    )(x2d, mixed2d, w_out_bf16, final_g.reshape(1, D_MODEL))


def _gelu(x):
    return 0.5 * x * (1.0 + lax.erf(x * (1.0 / math.sqrt(2.0))))


def _silu(x):
    return x * jax.nn.sigmoid(x)


_NN = (((1,), (0,)), ((), ()))
_NT = (((1,), (1,)), ((), ()))
_TN = (((0,), (0,)), ((), ()))


def _split_bf16(x, n):
    parts = []
    for _ in range(n - 1):
        p = x.astype(BF16)
        parts.append(p)
        x = x - p.astype(F32)
    parts.append(x.astype(BF16))
    return parts


def _dg(a, b, dims):
    return lax.dot_general(a, b, dims, preferred_element_type=F32)


def _mm(a, b, dims=_NN):
    return _dg(a.astype(BF16), b.astype(BF16), dims)


def _mm_3pass(a, b, dims=_NN):
    ah, al = _split_bf16(a, 2)
    bh, bl = _split_bf16(b, 2)
    return _dg(ah, bh, dims) + _dg(ah, bl, dims) + _dg(al, bh, dims)


def _mm_exact_rhs(a, b, n):
    bb = b.astype(BF16)
    return sum(_dg(p, bb, _NN) for p in _split_bf16(a, n))


def _mm_exact_lhs(a, b, n):
    ab = a.astype(BF16)
    return sum(_dg(ab, p, _NN) for p in _split_bf16(b, n))


def _iota(shape, dim):
    return lax.broadcasted_iota(jnp.int32, shape, dim)


def _mixer_kernel(z_ref, shift0_ref, wkv0_ref, mu_ref, lng_ref, lnb_ref, sguw_ref, sgub_ref,
                  wl_ref, bl_ref, kk_ref, ka_ref, rk_ref, gng_ref, gnb_ref,
                  mixed_ref, shift_out_ref, wkv_out_ref, *rest, tb, sgu_l, with_vn):
    if with_vn:
        vn_ref, s_scr, prev_scr = rest
    else:
        s_scr, prev_scr = rest
    c = WKV_CHUNK
    gw = GROUP_W
    t_idx = pl.program_id(1)
    n_t = pl.num_programs(1)

    bd_mask = _iota((gw, gw), 0) // B_HEAD_DIM == _iota((gw, gw), 1) // B_HEAD_DIM
    group_ones = bd_mask.astype(F32)

    @pl.when(t_idx == 0)
    def _init():
        prev_scr[...] = shift0_ref[0]
        tile_k = (_iota((B_HEAD_DIM, gw), 0) == _iota((B_HEAD_DIM, gw), 1) % B_HEAD_DIM).astype(F32)
        for g in range(N_GROUPS):
            s0 = wkv0_ref[0, g * gw:(g + 1) * gw, :]
            s_scr[g] = jnp.where(bd_mask, _mm_exact_rhs(s0, tile_k, 3), 0.0)

    def gsum(x):
        return jnp.concatenate(
            [_mm(x[:, g * gw:(g + 1) * gw], group_ones) for g in range(N_GROUPS)], axis=1)

    u = z_ref[0, :, 0:D_A]
    v = z_ref[0, :, D_A:2 * D_A]
    g_a = z_ref[0, :, 2 * D_A:3 * D_A]
    gv = _gelu(v)
    mean = jnp.mean(gv, axis=-1, keepdims=True)
    cen = gv - mean
    var = jnp.mean(cen * cen, axis=-1, keepdims=True)
    vn = cen * lax.rsqrt(var + LN_EPS) * lng_ref[...] + lnb_ref[...]
    if with_vn:
        vn_ref[0] = vn
    tril_l = _iota((sgu_l, sgu_l), 0) >= _iota((sgu_l, sgu_l), 1)
    vn_b = vn.astype(BF16)
    f_heads = []
    for h in range(A_HEADS):
        w = jnp.where(tril_l, sguw_ref[h], 0.0).astype(BF16)
        cols = slice(h * A_HEAD_DIM, (h + 1) * A_HEAD_DIM)
        f_chunks = [jnp.dot(w, vn_b[i * sgu_l:(i + 1) * sgu_l, cols], preferred_element_type=F32)
                    for i in range(tb // sgu_l)]
        f_heads.append(jnp.concatenate(f_chunks, axis=0) if len(f_chunks) > 1 else f_chunks[0])
    f = jnp.concatenate(f_heads, axis=1) + sgub_ref[...]
    out_a = _gelu(u) * f * _silu(g_a)

    zs = z_ref[0, :, 3 * D_A:3 * D_A + N_SHIFT]
    last_row = zs[tb - 1:tb, :]
    shifted = pltpu.roll(zs, 1, axis=0)
    z_prev = jnp.where(_iota((tb, N_SHIFT), 0) == 0, prev_scr[...], shifted)
    zmix = zs + mu_ref[...] * (z_prev - zs)
    prev_scr[...] = last_row
    shift_out_ref[0] = last_row

    r = zmix[:, 0:D_B]
    k = zmix[:, D_B:2 * D_B]
    vb = zmix[:, 2 * D_B:3 * D_B]
    lo = zmix[:, 3 * D_B:3 * D_B + 2 * LORA]
    lo_in = jnp.where(_iota((tb, 2 * LORA), 1) < LORA, jnp.tanh(lo), lo)
    da = _mm(lo_in, wl_ref[...]) + bl_ref[...]
    logw = -EXP_NEG_W_OFFSET * jax.nn.sigmoid(da[:, 0:D_B])
    a = jax.nn.sigmoid(da[:, D_B:2 * D_B])

    kk = k * kk_ref[...]
    kk = kk / jnp.maximum(jnp.sqrt(gsum(kk * kk)), 1e-12)
    k = k * (1.0 + (a - 1.0) * ka_ref[...])
    b = kk * a

    tril = _iota((tb, tb), 0) >= _iota((tb, tb), 1)
    same_chunk = _iota((tb, tb), 0) // c == _iota((tb, tb), 1) // c
    cum = _mm_exact_lhs(jnp.where(tril & same_chunk, 1.0, 0.0), logw, 2)

    t_w = _iota((c, gw), 0)
    s_w = _iota((c, gw), 1) % c
    strict_w = t_w > s_w
    incl_w = t_w >= s_w
    eye_w = (t_w == s_w).astype(F32)
    i_bd = _iota((gw, gw), 0) % c
    j_bd = _iota((gw, gw), 1) % c
    head_w = _iota((c, gw), 1) // B_HEAD_DIM

    def block_diag(x):
        return jnp.where(bd_mask, jnp.concatenate([x] * HEADS_PER_GROUP, axis=0), 0.0)

    n_chunks = tb // c
    chains = [(ci, g) for ci in range(n_chunks) for g in range(N_GROUPS)]
    lanes_of = lambda g: slice(g * gw, (g + 1) * gw)

    r_t, kk_t, b_t, k_t, b_h, k_h, v_c, g_last = [], [], [], [], [], [], [], []
    for ci in range(n_chunks):
        rows = slice(ci * c, (ci + 1) * c)
        cum_c = cum[rows]
        cum_last = cum_c[c - 1:c, :]
        g_inv = jnp.exp(-cum_c)
        g_hat = jnp.exp(cum_last - cum_c)
        r_t.append(r[rows] * jnp.exp(cum_c))
        kk_t.append(kk[rows] * jnp.exp(cum_c - logw[rows]))
        b_t.append(b[rows] * g_inv)
        k_t.append(k[rows] * g_inv)
        b_h.append(b[rows] * g_hat)
        k_h.append(k[rows] * g_hat)
        v_c.append(vb[rows])
        g_last.append(jnp.exp(cum_last))

    a_b, a_k, a_rb, a_rk, v_bd, l_bd, d_w = {}, {}, {}, {}, {}, {}, {}
    for ch in chains:
        ci, g = ch
        lanes = lanes_of(g)
        lhs = jnp.concatenate([kk_t[ci][:, lanes], r_t[ci][:, lanes]], axis=0)
        rb = jnp.concatenate(
            [jnp.where(head_w == h, b_t[ci][:, lanes], 0.0) for h in range(HEADS_PER_GROUP)]
            + [jnp.where(head_w == h, k_t[ci][:, lanes], 0.0) for h in range(HEADS_PER_GROUP)],
            axis=0)
        a_all = _mm(lhs, rb, _NT)
        a_b[ch], a_k[ch] = a_all[:c, :gw], a_all[:c, gw:]
        a_rb[ch], a_rk[ch] = a_all[c:, :gw], a_all[c:, gw:]
    for ch in chains:
        ci, g = ch
        l_w = jnp.where(strict_w, a_b[ch], 0.0)
        l_bd[ch] = block_diag(l_w)
        v_bd[ch] = block_diag(v_c[ci][:, lanes_of(g)])
        d_w[ch] = eye_w - jnp.where((t_w % 2 == 1) & (s_w == t_w - 1), l_w, 0.0)
    m = 2
    while m < c:
        off_mask = bd_mask & (i_bd // (2 * m) == j_bd // (2 * m)) \
            & ((i_bd // m) % 2 == 1) & ((j_bd // m) % 2 == 0)
        tmp = {ch: _mm(d_w[ch], jnp.where(off_mask, l_bd[ch], 0.0)) for ch in chains}
        d_w = {ch: d_w[ch] - _mm(tmp[ch], block_diag(d_w[ch])) for ch in chains}
        m *= 2
    akv = {ch: _mm(jnp.where(strict_w, a_k[ch], 0.0), v_bd[ch]) for ch in chains}
    dk = {ch: _mm(d_w[ch], block_diag(kk_t[ch[0]][:, lanes_of(ch[1])])) for ch in chains}
    u0 = {ch: -_mm(d_w[ch], block_diag(akv[ch])) for ch in chains}
    yv = {ch: _mm(jnp.where(incl_w, a_rk[ch], 0.0), v_bd[ch]) for ch in chains}

    y_rows = []
    for ci in range(n_chunks):
        s_old = [s_scr[g] for g in range(N_GROUPS)]
        pu = [_mm(jnp.concatenate([dk[(ci, g)], r_t[ci][:, lanes_of(g)]], axis=0), s_old[g], _NT)
              for g in range(N_GROUPS)]
        u = [u0[(ci, g)] - pu[g][:c] for g in range(N_GROUPS)]
        upd = [_mm_3pass(jnp.concatenate([u[g], v_c[ci][:, lanes_of(g)]], axis=0),
                         jnp.concatenate([b_h[ci][:, lanes_of(g)], k_h[ci][:, lanes_of(g)]], axis=0), _TN)
               for g in range(N_GROUPS)]
        for g in range(N_GROUPS):
            s_scr[g] = s_old[g] * g_last[ci][:, lanes_of(g)] + jnp.where(bd_mask, upd[g], 0.0)
        y_groups = [pu[g][c:] + yv[(ci, g)] + _mm(jnp.where(incl_w, a_rb[(ci, g)], 0.0), block_diag(u[g]))
                    for g in range(N_GROUPS)]
        y_rows.append(jnp.concatenate(y_groups, axis=1))
    y = jnp.concatenate(y_rows, axis=0) if len(y_rows) > 1 else y_rows[0]

    mu_y = gsum(y) * (1.0 / B_HEAD_DIM)
    y_cen = y - mu_y
    var_y = gsum(y_cen * y_cen) * (1.0 / B_HEAD_DIM)
    y = y_cen * lax.rsqrt(var_y + GN_EPS) * gng_ref[...] + gnb_ref[...]
    y = y + gsum(r * k * rk_ref[...]) * vb
    g_b = z_ref[0, :, 3 * D_A + N_SHIFT:D_PROJ]
    out_b = y * _silu(g_b)

    mixed_ref[0, :, 0:D_A] = out_a.astype(mixed_ref.dtype)
    mixed_ref[0, :, D_A:D_A + D_B] = out_b.astype(mixed_ref.dtype)

    @pl.when(t_idx == n_t - 1)
    def _final():
        untile_k = (_iota((gw, B_HEAD_DIM), 0) % B_HEAD_DIM == _iota((gw, B_HEAD_DIM), 1)).astype(F32)
        for g in range(N_GROUPS):
            wkv_out_ref[0, g * gw:(g + 1) * gw, :] = _mm_exact_rhs(s_scr[g], untile_k, 3)


def _mixer(z, shift0, wkv0, lw, with_vn):
    bsz, t, _ = z.shape
    sgu_l = min(t, SGU_CHUNK)
    tb = min(t, TIME_BLOCK)
    assert t % tb == 0 and tb % sgu_l == 0 and tb % WKV_CHUNK == 0
    n_t = t // tb
    row = lambda p: p.reshape(1, -1)
    sgu_w = lw["sgu_w"][:, :sgu_l, :sgu_l]
    sgu_b = jnp.tile(jnp.repeat(lw["sgu_b"][:, :sgu_l].T, A_HEAD_DIM, axis=1), (tb // sgu_l, 1))
    zeros = jnp.zeros((LORA, D_B), F32)
    w_lora = jnp.concatenate([jnp.concatenate([lw["w2"], zeros], axis=1),
                              jnp.concatenate([zeros, lw["a2"]], axis=1)], axis=0)
    b_lora = jnp.concatenate([lw["w0"], lw["a0"]]).reshape(1, 2 * D_B)
    const = lambda shape: pl.BlockSpec(shape, lambda b_, t_: (0,) * len(shape))
    in_specs = [
        pl.BlockSpec((1, tb, D_PROJ), lambda b_, t_: (b_, t_, 0)),
        pl.BlockSpec((1, 1, N_SHIFT), lambda b_, t_: (b_, 0, 0)),
        pl.BlockSpec((1, D_B, B_HEAD_DIM), lambda b_, t_: (b_, 0, 0)),
        const((1, N_SHIFT)), const((1, D_A)), const((1, D_A)),
        const((A_HEADS, sgu_l, sgu_l)), const((tb, D_A)),
        const((2 * LORA, 2 * D_B)), const((1, 2 * D_B)),
        const((1, D_B)), const((1, D_B)), const((1, D_B)), const((1, D_B)), const((1, D_B)),
    ]
    out_specs = [
        pl.BlockSpec((1, tb, D_A + D_B), lambda b_, t_: (b_, t_, 0)),
        pl.BlockSpec((1, 1, N_SHIFT), lambda b_, t_: (b_, 0, 0)),
        pl.BlockSpec((1, D_B, B_HEAD_DIM), lambda b_, t_: (b_, 0, 0)),
    ]
    out_shape = [
        jax.ShapeDtypeStruct((bsz, t, D_A + D_B), BF16),
        jax.ShapeDtypeStruct((bsz, 1, N_SHIFT), F32),
        jax.ShapeDtypeStruct((bsz, D_B, B_HEAD_DIM), F32),
    ]
    if with_vn:
        out_specs.append(pl.BlockSpec((1, tb, D_A), lambda b_, t_: (b_, t_, 0)))
        out_shape.append(jax.ShapeDtypeStruct((bsz, t, D_A), F32))
    outs = pl.pallas_call(
        functools.partial(_mixer_kernel, tb=tb, sgu_l=sgu_l, with_vn=with_vn),
        grid=(bsz, n_t),
        in_specs=in_specs,
        out_specs=out_specs,
        out_shape=out_shape,
        scratch_shapes=[pltpu.VMEM((N_GROUPS, GROUP_W, GROUP_W), F32),
                        pltpu.VMEM((1, N_SHIFT), F32)],
        compiler_params=pltpu.CompilerParams(
            dimension_semantics=("arbitrary", "arbitrary"), vmem_limit_bytes=VMEM_LIMIT),
        name="mixer",
    )(z, shift0, wkv0.reshape(bsz, D_B, B_HEAD_DIM),
      row(lw["shift_mu"]), row(lw["sgu_ln_g"]), row(lw["sgu_ln_b"]), sgu_w, sgu_b,
      w_lora, b_lora, row(lw["k_k"]), row(lw["k_a"]), row(lw["r_k"]), row(lw["gn_g"]), row(lw["gn_b"]))
    mixed, new_shift, wkv = outs[:3]
    vn = outs[3] if with_vn else None
    return mixed, new_shift, wkv.reshape(bsz, B_HEADS, B_HEAD_DIM, B_HEAD_DIM), vn


def _hybrid_layer(x, wkv0, shift0, lw, final_g, final_norm, with_vn):
    bsz, t, _ = x.shape
    x2d = x.reshape(bsz * t, D_MODEL)
    z = _proj_in(x2d, lw["norm_g"], lw["w_in"]).reshape(bsz, t, D_PROJ)
    mixed, new_shift, wkv, vn = _mixer(z, shift0, wkv0, lw, with_vn)
    y = _proj_out(x2d, mixed.reshape(bsz * t, D_MODEL), lw["w_out"], final_g, final_norm)
    return y.reshape(bsz, t, D_MODEL), wkv, new_shift, vn


_LAYER_PARAMS = ("norm_g", "w_in", "w_out", "sgu_ln_g", "sgu_ln_b", "sgu_w", "sgu_b", "shift_mu",
                 "w0", "w2", "a0", "a2", "k_k", "k_a", "r_k", "gn_g", "gn_b")


def kernel(x_prompt, x_sample, state_b_wkv, state_b_shift, norm_g, w_in, w_out, sgu_ln_g, sgu_ln_b,
           sgu_w, sgu_b, shift_mu, w0, w2, a0, a2, k_k, k_a, r_k, gn_g, gn_b, final_g):
    stacked = dict(zip(_LAYER_PARAMS, (norm_g, w_in, w_out, sgu_ln_g, sgu_ln_b, sgu_w, sgu_b, shift_mu,
                                       w0, w2, a0, a2, k_k, k_a, r_k, gn_g, gn_b)))
    depth = w_in.shape[0]
    bp = x_prompt.shape[0]
    zero_wkv = jnp.zeros((bp, B_HEADS, B_HEAD_DIM, B_HEAD_DIM), F32)
    zero_shift = jnp.zeros((bp, 1, N_SHIFT), F32)
    yp, ys = x_prompt, x_sample
    wkv_p, shift_p, wkv_s, shift_s, sgu_v_s = [], [], [], [], []
    for l in range(depth):
        lw = {name: p[l] for name, p in stacked.items()}
        lw["w_in"] = lw["w_in"].astype(BF16)
        lw["w_out"] = lw["w_out"].astype(BF16)
        last = l == depth - 1
        yp, s_p, sh_p, _ = _hybrid_layer(yp, zero_wkv, zero_shift, lw, final_g, last, False)
        ys, s_s, sh_s, vn_s = _hybrid_layer(ys, state_b_wkv[l], state_b_shift[l], lw, final_g, last, True)
        wkv_p.append(s_p); shift_p.append(sh_p)
        wkv_s.append(s_s); shift_s.append(sh_s); sgu_v_s.append(vn_s)
    return (yp, ys, jnp.stack(wkv_p), jnp.stack(shift_p), jnp.stack(wkv_s), jnp.stack(shift_s),
            jnp.stack(sgu_v_s))
```

```python
import functools
import math

import jax
import jax.numpy as jnp
from jax import lax
from jax.experimental import pallas as pl
from jax.experimental.pallas import tpu as pltpu

D_MODEL = 2048
D_A = 1024
D_B = 1024
SGU_CHUNK = 128
A_HEADS = 8
A_HEAD_DIM = D_A // A_HEADS
B_HEAD_DIM = 64
B_HEADS = D_B // B_HEAD_DIM
LORA = 64
N_SHIFT = 3 * D_B + 2 * LORA
D_PROJ_A = 3 * D_A
D_PROJ_B = N_SHIFT + D_B
NORM_EPS = 1e-6
LN_EPS = 1e-5
GN_EPS = 64e-5
EXP_NEG_W_OFFSET = math.exp(-0.5)

WKV_CHUNK = 64
HEADS_PER_GROUP = 4
GROUP_W = HEADS_PER_GROUP * B_HEAD_DIM
N_GROUPS = B_HEADS // HEADS_PER_GROUP

ROW_TILE = 512
STREAMS_PER_BLOCK = 4
IN_COL_TILE = 1408
VMEM_LIMIT = 56 * 1024 * 1024

F32 = jnp.float32
BF16 = jnp.bfloat16


def _rms_norm_rows(x, g):
    ms = jnp.mean(x * x, axis=-1, keepdims=True)
    return x * lax.rsqrt(ms + NORM_EPS) * g


def _gelu(x):
    return 0.5 * x * (1.0 + lax.erf(x * (1.0 / math.sqrt(2.0))))


def _silu(x):
    return x * jax.nn.sigmoid(x)


_NN = (((1,), (0,)), ((), ()))
_NT = (((1,), (1,)), ((), ()))
_TN = (((0,), (0,)), ((), ()))


def _split_bf16(x, n):
    parts = []
    for _ in range(n - 1):
        p = x.astype(BF16)
        parts.append(p)
        x = x - p.astype(F32)
    parts.append(x.astype(BF16))
    return parts


def _dg(a, b, dims):
    return lax.dot_general(a, b, dims, preferred_element_type=F32)


def _mm(a, b, dims=_NN):
    return _dg(a.astype(BF16), b.astype(BF16), dims)


def _mm_3pass(a, b, dims=_NN):
    ah, al = _split_bf16(a, 2)
    bh, bl = _split_bf16(b, 2)
    return _dg(ah, bh, dims) + _dg(ah, bl, dims) + _dg(al, bh, dims)


def _mm_exact_rhs(a, b, n):
    bb = b.astype(BF16)
    return sum(_dg(p, bb, _NN) for p in _split_bf16(a, n))


def _mm_exact_lhs(a, b, n):
    ab = a.astype(BF16)
    return sum(_dg(ab, p, _NN) for p in _split_bf16(b, n))


def _iota(shape, dim):
    return lax.broadcasted_iota(jnp.int32, shape, dim)


def _gmlp_kernel(x_ref, g_ref, w_ref, lng_ref, lnb_ref, sguw_ref, sgub_ref, outa_ref, *vn_refs,
                 rows, sgu_l):
    h = _rms_norm_rows(x_ref[...], g_ref[...]).astype(BF16)
    v = jnp.dot(h, w_ref[:, D_A:2 * D_A], preferred_element_type=F32)
    u = jnp.dot(h, w_ref[:, 0:D_A], preferred_element_type=F32)
    g_a = jnp.dot(h, w_ref[:, 2 * D_A:3 * D_A], preferred_element_type=F32)
    gv = _gelu(v)
    mean = jnp.mean(gv, axis=-1, keepdims=True)
    cen = gv - mean
    var = jnp.mean(cen * cen, axis=-1, keepdims=True)
    vn = cen * lax.rsqrt(var + LN_EPS) * lng_ref[...] + lnb_ref[...]
    if vn_refs:
        vn_refs[0][...] = vn
    tril_l = _iota((sgu_l, sgu_l), 0) >= _iota((sgu_l, sgu_l), 1)
    vn_b = vn.astype(BF16)
    f_heads = []
    for hd in range(A_HEADS):
        w = jnp.where(tril_l, sguw_ref[hd], 0.0).astype(BF16)
        cols = slice(hd * A_HEAD_DIM, (hd + 1) * A_HEAD_DIM)
        f_chunks = [jnp.dot(w, vn_b[i * sgu_l:(i + 1) * sgu_l, cols], preferred_element_type=F32)
                    for i in range(rows // sgu_l)]
        f_heads.append(jnp.concatenate(f_chunks, axis=0) if len(f_chunks) > 1 else f_chunks[0])
    f = jnp.concatenate(f_heads, axis=1) + sgub_ref[...]
    outa_ref[...] = (_gelu(u) * f * _silu(g_a)).astype(outa_ref.dtype)


def _gmlp(x2d, t, lw, with_vn):
    n_rows = x2d.shape[0]
    sgu_l = min(t, SGU_CHUNK)
    rows = ROW_TILE
    assert n_rows % rows == 0 and rows % sgu_l == 0 and t % sgu_l == 0
    row = lambda p: p.reshape(1, -1)
    sgu_w = lw["sgu_w"][:, :sgu_l, :sgu_l]
    sgu_b = jnp.tile(jnp.repeat(lw["sgu_b"][:, :sgu_l].T, A_HEAD_DIM, axis=1), (rows // sgu_l, 1))
    const = lambda shape: pl.BlockSpec(shape, lambda i: (0,) * len(shape))
    out_specs = [pl.BlockSpec((rows, D_A), lambda i: (i, 0))]
    out_shape = [jax.ShapeDtypeStruct((n_rows, D_A), BF16)]
    if with_vn:
        out_specs.append(pl.BlockSpec((rows, D_A), lambda i: (i, 0)))
        out_shape.append(jax.ShapeDtypeStruct((n_rows, D_A), F32))
    outs = pl.pallas_call(
        functools.partial(_gmlp_kernel, rows=rows, sgu_l=sgu_l),
        grid=(n_rows // rows,),
        in_specs=[
            pl.BlockSpec((rows, D_MODEL), lambda i: (i, 0)),
            const((1, D_MODEL)),
            pl.BlockSpec((D_MODEL, D_PROJ_A), lambda i: (0, 0), pipeline_mode=pl.Buffered(1)),
            const((1, D_A)), const((1, D_A)),
            const((A_HEADS, sgu_l, sgu_l)), const((rows, D_A)),
        ],
        out_specs=out_specs,
        out_shape=out_shape,
        compiler_params=pltpu.CompilerParams(
            dimension_semantics=("arbitrary",), vmem_limit_bytes=VMEM_LIMIT),
        name="gmlp",
    )(x2d, row(lw["norm_g"]), lw["w_in_a"], row(lw["sgu_ln_g"]), row(lw["sgu_ln_b"]), sgu_w, sgu_b)
    return outs[0], (outs[1] if with_vn else None)


def _proj_in_kernel(x_ref, g_ref, w_ref, z_ref):
    h = _rms_norm_rows(x_ref[...], g_ref[...])
    z_ref[...] = jnp.dot(h.astype(BF16), w_ref[...], preferred_element_type=F32)


def _proj_in(x2d, norm_g, w_in_b):
    rows = x2d.shape[0]
    assert rows % ROW_TILE == 0 and D_PROJ_B % IN_COL_TILE == 0
    return pl.pallas_call(
        _proj_in_kernel,
        grid=(D_PROJ_B // IN_COL_TILE, rows // ROW_TILE),
        in_specs=[
            pl.BlockSpec((ROW_TILE, D_MODEL), lambda j, i: (i, 0)),
            pl.BlockSpec((1, D_MODEL), lambda j, i: (0, 0)),
            pl.BlockSpec((D_MODEL, IN_COL_TILE), lambda j, i: (0, j)),
        ],
        out_specs=pl.BlockSpec((ROW_TILE, IN_COL_TILE), lambda j, i: (i, j)),
        out_shape=jax.ShapeDtypeStruct((rows, D_PROJ_B), F32),
        compiler_params=pltpu.CompilerParams(
            dimension_semantics=("arbitrary", "arbitrary"), vmem_limit_bytes=VMEM_LIMIT),
        name="proj_in",
    )(x2d, norm_g.reshape(1, D_MODEL), w_in_b)


def _mixer_kernel(z_ref, outa_ref, x_ref, shift0_ref, wkv0_ref, mu_ref, wl_ref, bl_ref,
                  kk_ref, ka_ref, rk_ref, gng_ref, gnb_ref, wout_ref, fg_ref,
                  y_ref, shift_out_ref, wkv_out_ref, s_scr, prev_scr, *, n_streams, final_norm):
    c = WKV_CHUNK
    gw = GROUP_W
    tb = n_streams * c
    t_idx = pl.program_id(1)
    n_t = pl.num_programs(1)

    bd_mask = _iota((gw, gw), 0) // B_HEAD_DIM == _iota((gw, gw), 1) // B_HEAD_DIM
    group_ones = bd_mask.astype(F32)

    @pl.when(t_idx == 0)
    def _init():
        prev_scr[...] = shift0_ref[...]
        tile_k = (_iota((B_HEAD_DIM, gw), 0) == _iota((B_HEAD_DIM, gw), 1) % B_HEAD_DIM).astype(F32)
        for si in range(n_streams):
            for g in range(N_GROUPS):
                s0 = wkv0_ref[si, g * gw:(g + 1) * gw, :]
                s_scr[si * N_GROUPS + g] = jnp.where(bd_mask, _mm_exact_rhs(s0, tile_k, 3), 0.0)

    def gsum(x):
        return jnp.concatenate(
            [_mm(x[:, g * gw:(g + 1) * gw], group_ones) for g in range(N_GROUPS)], axis=1)

    o_a = jnp.dot(outa_ref[...].reshape(tb, D_A), wout_ref[0:D_A, :], preferred_element_type=F32)

    first_step = _iota((c, N_SHIFT), 0) == 0
    zmix_streams = []
    for si in range(n_streams):
        zs = z_ref[si, :, 0:N_SHIFT]
        last_row = zs[c - 1:c, :]
        z_prev = jnp.where(first_step, prev_scr[si], pltpu.roll(zs, 1, axis=0))
        zmix_streams.append(zs + mu_ref[...] * (z_prev - zs))
        prev_scr[si] = last_row
        shift_out_ref[si] = last_row
    zmix = jnp.concatenate(zmix_streams, axis=0) if n_streams > 1 else zmix_streams[0]

    r = zmix[:, 0:D_B]
    k = zmix[:, D_B:2 * D_B]
    vb = zmix[:, 2 * D_B:3 * D_B]
    lo = zmix[:, 3 * D_B:3 * D_B + 2 * LORA]
    lo_in = jnp.where(_iota((tb, 2 * LORA), 1) < LORA, jnp.tanh(lo), lo)
    da = _mm(lo_in, wl_ref[...]) + bl_ref[...]
    logw = -EXP_NEG_W_OFFSET * jax.nn.sigmoid(da[:, 0:D_B])
    a = jax.nn.sigmoid(da[:, D_B:2 * D_B])

    kk = k * kk_ref[...]
    kk = kk / jnp.maximum(jnp.sqrt(gsum(kk * kk)), 1e-12)
    k = k * (1.0 + (a - 1.0) * ka_ref[...])
    b = kk * a

    tril = _iota((tb, tb), 0) >= _iota((tb, tb), 1)
    same_chunk = _iota((tb, tb), 0) // c == _iota((tb, tb), 1) // c
    cum = _mm_exact_lhs(jnp.where(tril & same_chunk, 1.0, 0.0), logw, 2)

    t_w = _iota((c, gw), 0)
    s_w = _iota((c, gw), 1) % c
    strict_w = t_w > s_w
    incl_w = t_w >= s_w
    eye_w = (t_w == s_w).astype(F32)
    i_bd = _iota((gw, gw), 0) % c
    j_bd = _iota((gw, gw), 1) % c
    head_w = _iota((c, gw), 1) // B_HEAD_DIM

    def block_diag(x):
        return jnp.where(bd_mask, jnp.concatenate([x] * HEADS_PER_GROUP, axis=0), 0.0)

    n_chunks = tb // c
    chains = [(ci, g) for ci in range(n_chunks) for g in range(N_GROUPS)]
    lanes_of = lambda g: slice(g * gw, (g + 1) * gw)

    r_t, kk_t, b_t, k_t, b_h, k_h, v_c, g_last = [], [], [], [], [], [], [], []
    for ci in range(n_chunks):
        rows = slice(ci * c, (ci + 1) * c)
        cum_c = cum[rows]
        cum_last = cum_c[c - 1:c, :]
        g_inv = jnp.exp(-cum_c)
        g_hat = jnp.exp(cum_last - cum_c)
        r_t.append(r[rows] * jnp.exp(cum_c))
        kk_t.append(kk[rows] * jnp.exp(cum_c - logw[rows]))
        b_t.append(b[rows] * g_inv)
        k_t.append(k[rows] * g_inv)
        b_h.append(b[rows] * g_hat)
        k_h.append(k[rows] * g_hat)
        v_c.append(vb[rows])
        g_last.append(jnp.exp(cum_last))

    a_b, a_k, a_rb, a_rk, v_bd, l_bd, d_w = {}, {}, {}, {}, {}, {}, {}
    for ch in chains:
        ci, g = ch
        lanes = lanes_of(g)
        lhs = jnp.concatenate([kk_t[ci][:, lanes], r_t[ci][:, lanes]], axis=0)
        rb = jnp.concatenate(
            [jnp.where(head_w == h, b_t[ci][:, lanes], 0.0) for h in range(HEADS_PER_GROUP)]
            + [jnp.where(head_w == h, k_t[ci][:, lanes], 0.0) for h in range(HEADS_PER_GROUP)],
            axis=0)
        a_all = _mm(lhs, rb, _NT)
        a_b[ch], a_k[ch] = a_all[:c, :gw], a_all[:c, gw:]
        a_rb[ch], a_rk[ch] = a_all[c:, :gw], a_all[c:, gw:]
    for ch in chains:
        ci, g = ch
        l_w = jnp.where(strict_w, a_b[ch], 0.0)
        l_bd[ch] = block_diag(l_w)
        v_bd[ch] = block_diag(v_c[ci][:, lanes_of(g)])
        d_w[ch] = eye_w - jnp.where((t_w % 2 == 1) & (s_w == t_w - 1), l_w, 0.0)
    m = 2
    while m < c:
        off_mask = bd_mask & (i_bd // (2 * m) == j_bd // (2 * m)) \
            & ((i_bd // m) % 2 == 1) & ((j_bd // m) % 2 == 0)
        tmp = {ch: _mm(d_w[ch], jnp.where(off_mask, l_bd[ch], 0.0)) for ch in chains}
        d_w = {ch: d_w[ch] - _mm(tmp[ch], block_diag(d_w[ch])) for ch in chains}
        m *= 2
    akv = {ch: _mm(jnp.where(strict_w, a_k[ch], 0.0), v_bd[ch]) for ch in chains}
    dk = {ch: _mm(d_w[ch], block_diag(kk_t[ch[0]][:, lanes_of(ch[1])])) for ch in chains}
    u0 = {ch: -_mm(d_w[ch], block_diag(akv[ch])) for ch in chains}
    yv = {ch: _mm(jnp.where(incl_w, a_rk[ch], 0.0), v_bd[ch]) for ch in chains}

    y_rows = []
    for ci in range(n_chunks):
        s_old = [s_scr[ci * N_GROUPS + g] for g in range(N_GROUPS)]
        pu = [_mm(jnp.concatenate([dk[(ci, g)], r_t[ci][:, lanes_of(g)]], axis=0), s_old[g], _NT)
              for g in range(N_GROUPS)]
        u = [u0[(ci, g)] - pu[g][:c] for g in range(N_GROUPS)]
        upd = [_mm_3pass(jnp.concatenate([u[g], v_c[ci][:, lanes_of(g)]], axis=0),
                         jnp.concatenate([b_h[ci][:, lanes_of(g)], k_h[ci][:, lanes_of(g)]], axis=0), _TN)
               for g in range(N_GROUPS)]
        for g in range(N_GROUPS):
            s_scr[ci * N_GROUPS + g] = (s_old[g] * g_last[ci][:, lanes_of(g)]
                                        + jnp.where(bd_mask, upd[g], 0.0))
        y_groups = [pu[g][c:] + yv[(ci, g)] + _mm(jnp.where(incl_w, a_rb[(ci, g)], 0.0), block_diag(u[g]))
                    for g in range(N_GROUPS)]
        y_rows.append(jnp.concatenate(y_groups, axis=1))
    y = jnp.concatenate(y_rows, axis=0) if len(y_rows) > 1 else y_rows[0]

    mu_y = gsum(y) * (1.0 / B_HEAD_DIM)
    y_cen = y - mu_y
    var_y = gsum(y_cen * y_cen) * (1.0 / B_HEAD_DIM)
    y = y_cen * lax.rsqrt(var_y + GN_EPS) * gng_ref[...] + gnb_ref[...]
    y = y + gsum(r * k * rk_ref[...]) * vb
    g_b = z_ref[:, :, N_SHIFT:D_PROJ_B].reshape(tb, D_B)
    out_b = (y * _silu(g_b)).astype(BF16)

    out = (x_ref[...].reshape(tb, D_MODEL) + o_a
           + jnp.dot(out_b, wout_ref[D_A:D_A + D_B, :], preferred_element_type=F32))
    if final_norm:
        out = _rms_norm_rows(out, fg_ref[...])
    y_ref[...] = out.reshape(n_streams, c, D_MODEL)

    @pl.when(t_idx == n_t - 1)
    def _final():
        untile_k = (_iota((gw, B_HEAD_DIM), 0) % B_HEAD_DIM == _iota((gw, B_HEAD_DIM), 1)).astype(F32)
        for si in range(n_streams):
            for g in range(N_GROUPS):
                wkv_out_ref[si, g * gw:(g + 1) * gw, :] = _mm_exact_rhs(
                    s_scr[si * N_GROUPS + g], untile_k, 3)


def _mixer(z, out_a, x, shift0, wkv0, lw, final_g, final_norm):
    bsz, t, _ = z.shape
    c = WKV_CHUNK
    ns = STREAMS_PER_BLOCK
    assert t % c == 0 and bsz % ns == 0
    row = lambda p: p.reshape(1, -1)
    zeros = jnp.zeros((LORA, D_B), F32)
    w_lora = jnp.concatenate([jnp.concatenate([lw["w2"], zeros], axis=1),
                              jnp.concatenate([zeros, lw["a2"]], axis=1)], axis=0)
    b_lora = jnp.concatenate([lw["w0"], lw["a0"]]).reshape(1, 2 * D_B)
    const = lambda shape: pl.BlockSpec(shape, lambda b_, t_: (0,) * len(shape))
    in_specs = [
        pl.BlockSpec((ns, c, D_PROJ_B), lambda b_, t_: (b_, t_, 0)),
        pl.BlockSpec((ns, c, D_A), lambda b_, t_: (b_, t_, 0)),
        pl.BlockSpec((ns, c, D_MODEL), lambda b_, t_: (b_, t_, 0)),
        pl.BlockSpec((ns, 1, N_SHIFT), lambda b_, t_: (b_, 0, 0)),
        pl.BlockSpec((ns, D_B, B_HEAD_DIM), lambda b_, t_: (b_, 0, 0)),
        const((1, N_SHIFT)),
        const((2 * LORA, 2 * D_B)), const((1, 2 * D_B)),
        const((1, D_B)), const((1, D_B)), const((1, D_B)), const((1, D_B)), const((1, D_B)),
        pl.BlockSpec((D_A + D_B, D_MODEL), lambda b_, t_: (0, 0), pipeline_mode=pl.Buffered(1)),
        const((1, D_MODEL)),
    ]
    out_specs = [
        pl.BlockSpec((ns, c, D_MODEL), lambda b_, t_: (b_, t_, 0)),
        pl.BlockSpec((ns, 1, N_SHIFT), lambda b_, t_: (b_, 0, 0)),
        pl.BlockSpec((ns, D_B, B_HEAD_DIM), lambda b_, t_: (b_, 0, 0)),
    ]
    out_shape = [
        jax.ShapeDtypeStruct((bsz, t, D_MODEL), F32),
        jax.ShapeDtypeStruct((bsz, 1, N_SHIFT), F32),
        jax.ShapeDtypeStruct((bsz, D_B, B_HEAD_DIM), F32),
    ]
    y, new_shift, wkv = pl.pallas_call(
        functools.partial(_mixer_kernel, n_streams=ns, final_norm=final_norm),
        grid=(bsz // ns, t // c),
        in_specs=in_specs,
        out_specs=out_specs,
        out_shape=out_shape,
        scratch_shapes=[pltpu.VMEM((ns * N_GROUPS, GROUP_W, GROUP_W), F32),
                        pltpu.VMEM((ns, 1, N_SHIFT), F32)],
        compiler_params=pltpu.CompilerParams(
            dimension_semantics=("arbitrary", "arbitrary"), vmem_limit_bytes=VMEM_LIMIT),
        name="mixer",
    )(z, out_a, x, shift0, wkv0.reshape(bsz, D_B, B_HEAD_DIM),
      row(lw["shift_mu"]), w_lora, b_lora,
      row(lw["k_k"]), row(lw["k_a"]), row(lw["r_k"]), row(lw["gn_g"]), row(lw["gn_b"]),
      lw["w_out"], row(final_g))
    return y, new_shift, wkv.reshape(bsz, B_HEADS, B_HEAD_DIM, B_HEAD_DIM)


def _hybrid_layer(x, wkv0, shift0, lw, final_g, final_norm, with_vn):
    bsz, t, _ = x.shape
    x2d = x.reshape(bsz * t, D_MODEL)
    out_a, vn = _gmlp(x2d, t, lw, with_vn)
    z_b = _proj_in(x2d, lw["norm_g"], lw["w_in_b"])
    y, new_shift, wkv = _mixer(z_b.reshape(bsz, t, D_PROJ_B), out_a.reshape(bsz, t, D_A), x,
                               shift0, wkv0, lw, final_g, final_norm)
    if with_vn:
        vn = vn.reshape(bsz, t, D_A)
    return y, wkv, new_shift, vn


_LAYER_PARAMS = ("norm_g", "w_in", "w_out", "sgu_ln_g", "sgu_ln_b", "sgu_w", "sgu_b", "shift_mu",
                 "w0", "w2", "a0", "a2", "k_k", "k_a", "r_k", "gn_g", "gn_b")


def kernel(x_prompt, x_sample, state_b_wkv, state_b_shift, norm_g, w_in, w_out, sgu_ln_g, sgu_ln_b,
           sgu_w, sgu_b, shift_mu, w0, w2, a0, a2, k_k, k_a, r_k, gn_g, gn_b, final_g):
    stacked = dict(zip(_LAYER_PARAMS, (norm_g, w_in, w_out, sgu_ln_g, sgu_ln_b, sgu_w, sgu_b, shift_mu,
                                       w0, w2, a0, a2, k_k, k_a, r_k, gn_g, gn_b)))
    depth = w_in.shape[0]
    bp = x_prompt.shape[0]
    zero_wkv = jnp.zeros((bp, B_HEADS, B_HEAD_DIM, B_HEAD_DIM), F32)
    zero_shift = jnp.zeros((bp, 1, N_SHIFT), F32)
    yp, ys = x_prompt, x_sample
    wkv_p, shift_p, wkv_s, shift_s, sgu_v_s = [], [], [], [], []
    for l in range(depth):
        lw = {name: p[l] for name, p in stacked.items()}
        lw["w_in_a"] = lw["w_in"][:, :D_PROJ_A].astype(BF16)
        lw["w_in_b"] = lw["w_in"][:, D_PROJ_A:].astype(BF16)
        lw["w_out"] = lw["w_out"].astype(BF16)
        last = l == depth - 1
        yp, s_p, sh_p, _ = _hybrid_layer(yp, zero_wkv, zero_shift, lw, final_g, last, False)
        ys, s_s, sh_s, vn_s = _hybrid_layer(ys, state_b_wkv[l], state_b_shift[l], lw, final_g, last, True)
        wkv_p.append(s_p); shift_p.append(sh_p)
        wkv_s.append(s_s); shift_s.append(sh_s); sgu_v_s.append(vn_s)
    return (yp, ys, jnp.stack(wkv_p), jnp.stack(shift_p), jnp.stack(wkv_s), jnp.stack(shift_s),
            jnp.stack(sgu_v_s))
```

```python
import functools
import math

import jax
import jax.numpy as jnp
from jax import lax
from jax.experimental import pallas as pl
from jax.experimental.pallas import tpu as pltpu

D_MODEL = 2048
D_A = 1024
D_B = 1024
SGU_CHUNK = 128
A_HEADS = 8
A_HEAD_DIM = D_A // A_HEADS
B_HEAD_DIM = 64
B_HEADS = D_B // B_HEAD_DIM
LORA = 64
N_SHIFT = 3 * D_B + 2 * LORA
D_PROJ_A = 3 * D_A
D_PROJ_B = N_SHIFT + D_B
NORM_EPS = 1e-6
LN_EPS = 1e-5
GN_EPS = 64e-5
EXP_NEG_W_OFFSET = math.exp(-0.5)

WKV_CHUNK = 64
HEADS_PER_GROUP = 4
GROUP_W = HEADS_PER_GROUP * B_HEAD_DIM
N_GROUPS = B_HEADS // HEADS_PER_GROUP

ROW_TILE = 512
STREAMS_PER_BLOCK = 4
IN_COL_TILE = 1408
VMEM_LIMIT = 56 * 1024 * 1024

F32 = jnp.float32
BF16 = jnp.bfloat16


def _rms_norm_rows(x, g):
    ms = jnp.mean(x * x, axis=-1, keepdims=True)
    return x * lax.rsqrt(ms + NORM_EPS) * g


def _gelu(x):
    return 0.5 * x * (1.0 + lax.erf(x * (1.0 / math.sqrt(2.0))))


def _silu(x):
    return x * jax.nn.sigmoid(x)


_NN = (((1,), (0,)), ((), ()))
_NT = (((1,), (1,)), ((), ()))
_TN = (((0,), (0,)), ((), ()))


def _split_bf16(x, n):
    parts = []
    for _ in range(n - 1):
        p = x.astype(BF16)
        parts.append(p)
        x = x - p.astype(F32)
    parts.append(x.astype(BF16))
    return parts


def _dg(a, b, dims):
    return lax.dot_general(a, b, dims, preferred_element_type=F32)


def _mm(a, b, dims=_NN):
    return _dg(a.astype(BF16), b.astype(BF16), dims)


def _mm_exact_rhs(a, b, n):
    bb = b.astype(BF16)
    return sum(_dg(p, bb, _NN) for p in _split_bf16(a, n))


def _mm_exact_lhs(a, b, n):
    ab = a.astype(BF16)
    return sum(_dg(ab, p, _NN) for p in _split_bf16(b, n))


def _iota(shape, dim):
    return lax.broadcasted_iota(jnp.int32, shape, dim)


def _gmlp_kernel(x_ref, g_ref, w_ref, lng_ref, lnb_ref, sguw_ref, sgub_ref, outa_ref, *vn_refs,
                 rows, sgu_l):
    h = _rms_norm_rows(x_ref[...], g_ref[...]).astype(BF16)
    v = jnp.dot(h, w_ref[:, D_A:2 * D_A], preferred_element_type=F32)
    u = jnp.dot(h, w_ref[:, 0:D_A], preferred_element_type=F32)
    g_a = jnp.dot(h, w_ref[:, 2 * D_A:3 * D_A], preferred_element_type=F32)
    gv = _gelu(v)
    mean = jnp.mean(gv, axis=-1, keepdims=True)
    cen = gv - mean
    var = jnp.mean(cen * cen, axis=-1, keepdims=True)
    vn = cen * lax.rsqrt(var + LN_EPS) * lng_ref[...] + lnb_ref[...]
    if vn_refs:
        vn_refs[0][...] = vn
    tril_l = _iota((sgu_l, sgu_l), 0) >= _iota((sgu_l, sgu_l), 1)
    vn_b = vn.astype(BF16)
    f_heads = []
    for hd in range(A_HEADS):
        w = jnp.where(tril_l, sguw_ref[hd], 0.0).astype(BF16)
        cols = slice(hd * A_HEAD_DIM, (hd + 1) * A_HEAD_DIM)
        f_chunks = [jnp.dot(w, vn_b[i * sgu_l:(i + 1) * sgu_l, cols], preferred_element_type=F32)
                    for i in range(rows // sgu_l)]
        f_heads.append(jnp.concatenate(f_chunks, axis=0) if len(f_chunks) > 1 else f_chunks[0])
    f = jnp.concatenate(f_heads, axis=1) + sgub_ref[...]
    outa_ref[...] = (_gelu(u) * f * _silu(g_a)).astype(outa_ref.dtype)


def _gmlp(x2d, t, lw, with_vn):
    n_rows = x2d.shape[0]
    sgu_l = min(t, SGU_CHUNK)
    rows = ROW_TILE
    assert n_rows % rows == 0 and rows % sgu_l == 0 and t % sgu_l == 0
    row = lambda p: p.reshape(1, -1)
    sgu_w = lw["sgu_w"][:, :sgu_l, :sgu_l]
    sgu_b = jnp.tile(jnp.repeat(lw["sgu_b"][:, :sgu_l].T, A_HEAD_DIM, axis=1), (rows // sgu_l, 1))
    const = lambda shape: pl.BlockSpec(shape, lambda i: (0,) * len(shape))
    out_specs = [pl.BlockSpec((rows, D_A), lambda i: (i, 0))]
    out_shape = [jax.ShapeDtypeStruct((n_rows, D_A), BF16)]
    if with_vn:
        out_specs.append(pl.BlockSpec((rows, D_A), lambda i: (i, 0)))
        out_shape.append(jax.ShapeDtypeStruct((n_rows, D_A), F32))
    outs = pl.pallas_call(
        functools.partial(_gmlp_kernel, rows=rows, sgu_l=sgu_l),
        grid=(n_rows // rows,),
        in_specs=[
            pl.BlockSpec((rows, D_MODEL), lambda i: (i, 0)),
            const((1, D_MODEL)),
            pl.BlockSpec((D_MODEL, D_PROJ_A), lambda i: (0, 0), pipeline_mode=pl.Buffered(1)),
            const((1, D_A)), const((1, D_A)),
            const((A_HEADS, sgu_l, sgu_l)), const((rows, D_A)),
        ],
        out_specs=out_specs,
        out_shape=out_shape,
        compiler_params=pltpu.CompilerParams(
            dimension_semantics=("arbitrary",), vmem_limit_bytes=VMEM_LIMIT),
        name="gmlp",
    )(x2d, row(lw["norm_g"]), lw["w_in_a"], row(lw["sgu_ln_g"]), row(lw["sgu_ln_b"]), sgu_w, sgu_b)
    return outs[0], (outs[1] if with_vn else None)


def _proj_in_kernel(x_ref, g_ref, w_ref, z_ref):
    h = _rms_norm_rows(x_ref[...], g_ref[...])
    z_ref[...] = jnp.dot(h.astype(BF16), w_ref[...], preferred_element_type=F32)


def _proj_in(x2d, norm_g, w_in_b):
    rows = x2d.shape[0]
    assert rows % ROW_TILE == 0 and D_PROJ_B % IN_COL_TILE == 0
    return pl.pallas_call(
        _proj_in_kernel,
        grid=(D_PROJ_B // IN_COL_TILE, rows // ROW_TILE),
        in_specs=[
            pl.BlockSpec((ROW_TILE, D_MODEL), lambda j, i: (i, 0)),
            pl.BlockSpec((1, D_MODEL), lambda j, i: (0, 0)),
            pl.BlockSpec((D_MODEL, IN_COL_TILE), lambda j, i: (0, j)),
        ],
        out_specs=pl.BlockSpec((ROW_TILE, IN_COL_TILE), lambda j, i: (i, j)),
        out_shape=jax.ShapeDtypeStruct((rows, D_PROJ_B), F32),
        compiler_params=pltpu.CompilerParams(
            dimension_semantics=("arbitrary", "arbitrary"), vmem_limit_bytes=VMEM_LIMIT),
        name="proj_in",
    )(x2d, norm_g.reshape(1, D_MODEL), w_in_b)


def _run_interleaved(*gens):
    live = list(gens)
    while live:
        for gen in list(live):
            try:
                next(gen)
            except StopIteration:
                live.remove(gen)


def _mixer_kernel(z_ref, outa_ref, x_ref, shift0_ref, wkv0_ref, mu_ref, wl_ref, bl_ref,
                  kk_ref, ka_ref, rk_ref, gng_ref, gnb_ref, wout_ref, fg_ref,
                  y_ref, shift_out_ref, wkv_out_ref, s_scr, prev_scr, *, n_streams, final_norm):
    c = WKV_CHUNK
    gw = GROUP_W
    tb = n_streams * c
    t_idx = pl.program_id(1)
    n_t = pl.num_programs(1)

    bd_mask = _iota((gw, gw), 0) // B_HEAD_DIM == _iota((gw, gw), 1) // B_HEAD_DIM
    group_ones = bd_mask.astype(F32)

    @pl.when(t_idx == 0)
    def _init():
        prev_scr[...] = shift0_ref[...]
        tile_k = (_iota((B_HEAD_DIM, gw), 0) == _iota((B_HEAD_DIM, gw), 1) % B_HEAD_DIM).astype(F32)
        for si in range(n_streams):
            for g in range(N_GROUPS):
                s0 = wkv0_ref[si, g * gw:(g + 1) * gw, :]
                s_scr[si * N_GROUPS + g] = jnp.where(bd_mask, _mm_exact_rhs(s0, tile_k, 3), 0.0)

    def gsum(x):
        return jnp.concatenate(
            [_mm(x[:, g * gw:(g + 1) * gw], group_ones) for g in range(N_GROUPS)], axis=1)

    t_w = _iota((c, gw), 0)
    s_w = _iota((c, gw), 1) % c
    strict_w = t_w > s_w
    incl_w = t_w >= s_w
    eye_w = (t_w == s_w).astype(F32)
    i_bd = _iota((gw, gw), 0) % c
    j_bd = _iota((gw, gw), 1) % c
    head_w = _iota((c, gw), 1) // B_HEAD_DIM
    first_step = _iota((c, N_SHIFT), 0) == 0
    lanes_of = lambda g: slice(g * gw, (g + 1) * gw)

    def block_diag(x):
        return jnp.where(bd_mask, jnp.concatenate([x] * HEADS_PER_GROUP, axis=0), 0.0)

    ops = {}
    pre = {}
    y_of = {}

    def prep(streams):
        nr = len(streams) * c
        parts = []
        for si in streams:
            zs = z_ref[si, :, 0:N_SHIFT]
            last_row = zs[c - 1:c, :]
            z_prev = jnp.where(first_step, prev_scr[si], pltpu.roll(zs, 1, axis=0))
            parts.append(zs + mu_ref[...] * (z_prev - zs))
            prev_scr[si] = last_row
            shift_out_ref[si] = last_row
            yield
        zmix = jnp.concatenate(parts, axis=0) if len(parts) > 1 else parts[0]
        r = zmix[:, 0:D_B]
        k = zmix[:, D_B:2 * D_B]
        vb = zmix[:, 2 * D_B:3 * D_B]
        lo = zmix[:, 3 * D_B:3 * D_B + 2 * LORA]
        lo_in = jnp.where(_iota((nr, 2 * LORA), 1) < LORA, jnp.tanh(lo), lo)
        da = _mm(lo_in, wl_ref[...]) + bl_ref[...]
        yield
        logw = -EXP_NEG_W_OFFSET * jax.nn.sigmoid(da[:, 0:D_B])
        a = jax.nn.sigmoid(da[:, D_B:2 * D_B])
        yield
        kk = k * kk_ref[...]
        norm2 = gsum(kk * kk)
        yield
        kk = kk / jnp.maximum(jnp.sqrt(norm2), 1e-12)
        k = k * (1.0 + (a - 1.0) * ka_ref[...])
        b = kk * a
        yield
        tril = _iota((nr, nr), 0) >= _iota((nr, nr), 1)
        same_chunk = _iota((nr, nr), 0) // c == _iota((nr, nr), 1) // c
        cum = _mm_exact_lhs(jnp.where(tril & same_chunk, 1.0, 0.0), logw, 2)
        bonus = gsum(r * k * rk_ref[...]) * vb
        yield
        for j, si in enumerate(streams):
            rows = slice(j * c, (j + 1) * c)
            cum_c = cum[rows]
            cum_last = cum_c[c - 1:c, :]
            g_inv = jnp.exp(-cum_c)
            g_hat = jnp.exp(cum_last - cum_c)
            ops[si] = dict(
                r_t=r[rows] * jnp.exp(cum_c),
                kk_t=kk[rows] * jnp.exp(cum_c - logw[rows]),
                b_t=b[rows] * g_inv, k_t=k[rows] * g_inv,
                b_h=b[rows] * g_hat, k_h=k[rows] * g_hat,
                v=vb[rows], g_last=jnp.exp(cum_last), bonus=bonus[rows])
            yield

    def chains(streams):
        keys = [(si, g) for si in streams for g in range(N_GROUPS)]
        a_b, a_k, a_rb, a_rk, v_bd, l_bd, d_w = {}, {}, {}, {}, {}, {}, {}
        for ch in keys:
            si, g = ch
            lanes = lanes_of(g)
            o = ops[si]
            lhs = jnp.concatenate([o["kk_t"][:, lanes], o["r_t"][:, lanes]], axis=0)
            rb = jnp.concatenate(
                [jnp.where(head_w == h, o["b_t"][:, lanes], 0.0) for h in range(HEADS_PER_GROUP)]
                + [jnp.where(head_w == h, o["k_t"][:, lanes], 0.0) for h in range(HEADS_PER_GROUP)],
                axis=0)
            a_all = _mm(lhs, rb, _NT)
            a_b[ch], a_k[ch] = a_all[:c, :gw], a_all[:c, gw:]
            a_rb[ch], a_rk[ch] = a_all[c:, :gw], a_all[c:, gw:]
        yield
        for ch in keys:
            si, g = ch
            l_w = jnp.where(strict_w, a_b[ch], 0.0).astype(BF16)
            l_bd[ch] = block_diag(l_w)
            v_bd[ch] = block_diag(ops[si]["v"][:, lanes_of(g)].astype(BF16))
            d_w[ch] = (eye_w - jnp.where((t_w % 2 == 1) & (s_w == t_w - 1), l_w, 0.0).astype(F32)).astype(BF16)
        yield
        m = 2
        while m < c:
            off_mask = bd_mask & (i_bd // (2 * m) == j_bd // (2 * m)) \
                & ((i_bd // m) % 2 == 1) & ((j_bd // m) % 2 == 0)
            tmp = {ch: _mm(d_w[ch], jnp.where(off_mask, l_bd[ch], 0.0)).astype(BF16) for ch in keys}
            yield
            d_w = {ch: (d_w[ch].astype(F32) - _mm(tmp[ch], block_diag(d_w[ch]))).astype(BF16) for ch in keys}
            yield
            m *= 2
        akv = {ch: _mm(jnp.where(strict_w, a_k[ch], 0.0), v_bd[ch]) for ch in keys}
        dk = {ch: _mm(d_w[ch], block_diag(ops[ch[0]]["kk_t"][:, lanes_of(ch[1])])) for ch in keys}
        yield
        u0 = {ch: -_mm(d_w[ch], block_diag(akv[ch])) for ch in keys}
        yv = {ch: _mm(jnp.where(incl_w, a_rk[ch], 0.0), v_bd[ch]) for ch in keys}
        for ch in keys:
            pre[ch] = dict(dk=dk[ch], u0=u0[ch], yv=yv[ch], a_rb=a_rb[ch])
        yield

    def state_dep(streams):
        for si in streams:
            o = ops[si]
            s_old = [s_scr[si * N_GROUPS + g] for g in range(N_GROUPS)]
            pu = [_mm(jnp.concatenate([pre[(si, g)]["dk"], o["r_t"][:, lanes_of(g)]], axis=0), s_old[g], _NT)
                  for g in range(N_GROUPS)]
            u = [pre[(si, g)]["u0"] - pu[g][:c] for g in range(N_GROUPS)]
            yield
            upd = [_mm(jnp.concatenate([u[g], o["v"][:, lanes_of(g)]], axis=0),
                       jnp.concatenate([o["b_h"][:, lanes_of(g)], o["k_h"][:, lanes_of(g)]], axis=0), _TN)
                   for g in range(N_GROUPS)]
            for g in range(N_GROUPS):
                s_scr[si * N_GROUPS + g] = (s_old[g] * o["g_last"][:, lanes_of(g)]
                                            + jnp.where(bd_mask, upd[g], 0.0))
            yield
            y_groups = [pu[g][c:] + pre[(si, g)]["yv"]
                        + _mm(jnp.where(incl_w, pre[(si, g)]["a_rb"], 0.0), block_diag(u[g]))
                        for g in range(N_GROUPS)]
            y_of[si] = jnp.concatenate(y_groups, axis=1)
            yield

    half = max(n_streams // 2, 1)
    first, second = list(range(half)), list(range(half, n_streams))
    _run_interleaved(prep(first))
    o_a = jnp.dot(outa_ref[...].reshape(tb, D_A), wout_ref[0:D_A, :], preferred_element_type=F32)
    _run_interleaved(chains(first), prep(second))
    _run_interleaved(state_dep(first), chains(second))
    _run_interleaved(state_dep(second))

    y = jnp.concatenate([y_of[si] for si in range(n_streams)], axis=0) if n_streams > 1 else y_of[0]
    bonus = (jnp.concatenate([ops[si]["bonus"] for si in range(n_streams)], axis=0)
             if n_streams > 1 else ops[0]["bonus"])
    mu_y = gsum(y) * (1.0 / B_HEAD_DIM)
    y_cen = y - mu_y
    var_y = gsum(y_cen * y_cen) * (1.0 / B_HEAD_DIM)
    y = y_cen * lax.rsqrt(var_y + GN_EPS) * gng_ref[...] + gnb_ref[...]
    y = y + bonus
    g_b = z_ref[:, :, N_SHIFT:D_PROJ_B].reshape(tb, D_B)
    out_b = (y * _silu(g_b)).astype(BF16)

    out = (x_ref[...].reshape(tb, D_MODEL) + o_a
           + jnp.dot(out_b, wout_ref[D_A:D_A + D_B, :], preferred_element_type=F32))
    if final_norm:
        out = _rms_norm_rows(out, fg_ref[...])
    y_ref[...] = out.reshape(n_streams, c, D_MODEL)

    @pl.when(t_idx == n_t - 1)
    def _final():
        untile_k = (_iota((gw, B_HEAD_DIM), 0) % B_HEAD_DIM == _iota((gw, B_HEAD_DIM), 1)).astype(F32)
        for si in range(n_streams):
            for g in range(N_GROUPS):
                wkv_out_ref[si, g * gw:(g + 1) * gw, :] = _mm_exact_rhs(
                    s_scr[si * N_GROUPS + g], untile_k, 3)


def _mixer(z, out_a, x, shift0, wkv0, lw, final_g, final_norm):
    bsz, t, _ = z.shape
    c = WKV_CHUNK
    ns = STREAMS_PER_BLOCK
    assert t % c == 0 and bsz % ns == 0
    row = lambda p: p.reshape(1, -1)
    zeros = jnp.zeros((LORA, D_B), F32)
    w_lora = jnp.concatenate([jnp.concatenate([lw["w2"], zeros], axis=1),
                              jnp.concatenate([zeros, lw["a2"]], axis=1)], axis=0)
    b_lora = jnp.concatenate([lw["w0"], lw["a0"]]).reshape(1, 2 * D_B)
    const = lambda shape: pl.BlockSpec(shape, lambda b_, t_: (0,) * len(shape))
    in_specs = [
        pl.BlockSpec((ns, c, D_PROJ_B), lambda b_, t_: (b_, t_, 0)),
        pl.BlockSpec((ns, c, D_A), lambda b_, t_: (b_, t_, 0)),
        pl.BlockSpec((ns, c, D_MODEL), lambda b_, t_: (b_, t_, 0)),
        pl.BlockSpec((ns, 1, N_SHIFT), lambda b_, t_: (b_, 0, 0)),
        pl.BlockSpec((ns, D_B, B_HEAD_DIM), lambda b_, t_: (b_, 0, 0)),
        const((1, N_SHIFT)),
        const((2 * LORA, 2 * D_B)), const((1, 2 * D_B)),
        const((1, D_B)), const((1, D_B)), const((1, D_B)), const((1, D_B)), const((1, D_B)),
        pl.BlockSpec((D_A + D_B, D_MODEL), lambda b_, t_: (0, 0), pipeline_mode=pl.Buffered(1)),
        const((1, D_MODEL)),
    ]
    out_specs = [
        pl.BlockSpec((ns, c, D_MODEL), lambda b_, t_: (b_, t_, 0)),
        pl.BlockSpec((ns, 1, N_SHIFT), lambda b_, t_: (b_, 0, 0)),
        pl.BlockSpec((ns, D_B, B_HEAD_DIM), lambda b_, t_: (b_, 0, 0)),
    ]
    out_shape = [
        jax.ShapeDtypeStruct((bsz, t, D_MODEL), F32),
        jax.ShapeDtypeStruct((bsz, 1, N_SHIFT), F32),
        jax.ShapeDtypeStruct((bsz, D_B, B_HEAD_DIM), F32),
    ]
    y, new_shift, wkv = pl.pallas_call(
        functools.partial(_mixer_kernel, n_streams=ns, final_norm=final_norm),
        grid=(bsz // ns, t // c),
        in_specs=in_specs,
        out_specs=out_specs,
        out_shape=out_shape,
        scratch_shapes=[pltpu.VMEM((ns * N_GROUPS, GROUP_W, GROUP_W), F32),
                        pltpu.VMEM((ns, 1, N_SHIFT), F32)],
        compiler_params=pltpu.CompilerParams(
            dimension_semantics=("arbitrary", "arbitrary"), vmem_limit_bytes=VMEM_LIMIT),
        name="mixer",
    )(z, out_a, x, shift0, wkv0.reshape(bsz, D_B, B_HEAD_DIM),
      row(lw["shift_mu"]), w_lora, b_lora,
      row(lw["k_k"]), row(lw["k_a"]), row(lw["r_k"]), row(lw["gn_g"]), row(lw["gn_b"]),
      lw["w_out"], row(final_g))
    return y, new_shift, wkv.reshape(bsz, B_HEADS, B_HEAD_DIM, B_HEAD_DIM)


def _hybrid_layer(x, wkv0, shift0, lw, final_g, final_norm, with_vn):
    bsz, t, _ = x.shape
    x2d = x.reshape(bsz * t, D_MODEL)
    out_a, vn = _gmlp(x2d, t, lw, with_vn)
    z_b = _proj_in(x2d, lw["norm_g"], lw["w_in_b"])
    y, new_shift, wkv = _mixer(z_b.reshape(bsz, t, D_PROJ_B), out_a.reshape(bsz, t, D_A), x,
                               shift0, wkv0, lw, final_g, final_norm)
    if with_vn:
        vn = vn.reshape(bsz, t, D_A)
    return y, wkv, new_shift, vn


_LAYER_PARAMS = ("norm_g", "w_in", "w_out", "sgu_ln_g", "sgu_ln_b", "sgu_w", "sgu_b", "shift_mu",
                 "w0", "w2", "a0", "a2", "k_k", "k_a", "r_k", "gn_g", "gn_b")


def kernel(x_prompt, x_sample, state_b_wkv, state_b_shift, norm_g, w_in, w_out, sgu_ln_g, sgu_ln_b,
           sgu_w, sgu_b, shift_mu, w0, w2, a0, a2, k_k, k_a, r_k, gn_g, gn_b, final_g):
    stacked = dict(zip(_LAYER_PARAMS, (norm_g, w_in, w_out, sgu_ln_g, sgu_ln_b, sgu_w, sgu_b, shift_mu,
                                       w0, w2, a0, a2, k_k, k_a, r_k, gn_g, gn_b)))
    depth = w_in.shape[0]
    bp = x_prompt.shape[0]
    zero_wkv = jnp.zeros((bp, B_HEADS, B_HEAD_DIM, B_HEAD_DIM), F32)
    zero_shift = jnp.zeros((bp, 1, N_SHIFT), F32)
    yp, ys = x_prompt, x_sample
    wkv_p, shift_p, wkv_s, shift_s, sgu_v_s = [], [], [], [], []
    for l in range(depth):
        lw = {name: p[l] for name, p in stacked.items()}
        lw["w_in_a"] = lw["w_in"][:, :D_PROJ_A].astype(BF16)
        lw["w_in_b"] = lw["w_in"][:, D_PROJ_A:].astype(BF16)
        lw["w_out"] = lw["w_out"].astype(BF16)
        last = l == depth - 1
        yp, s_p, sh_p, _ = _hybrid_layer(yp, zero_wkv, zero_shift, lw, final_g, last, False)
        ys, s_s, sh_s, vn_s = _hybrid_layer(ys, state_b_wkv[l], state_b_shift[l], lw, final_g, last, True)
        wkv_p.append(s_p); shift_p.append(sh_p)
        wkv_s.append(s_s); shift_s.append(sh_s); sgu_v_s.append(vn_s)
    return (yp, ys, jnp.stack(wkv_p), jnp.stack(shift_p), jnp.stack(wkv_s), jnp.stack(shift_s),
            jnp.stack(sgu_v_s))
```

```python
import functools
import math

import jax
import jax.numpy as jnp
from jax import lax
from jax.experimental import pallas as pl
from jax.experimental.pallas import tpu as pltpu

D_MODEL = 2048
D_A = 1024
D_B = 1024
SGU_CHUNK = 128
A_HEADS = 8
A_HEAD_DIM = D_A // A_HEADS
B_HEAD_DIM = 64
B_HEADS = D_B // B_HEAD_DIM
LORA = 64
N_SHIFT = 3 * D_B + 2 * LORA
D_PROJ_A = 3 * D_A
D_PROJ_B = N_SHIFT + D_B
NORM_EPS = 1e-6
LN_EPS = 1e-5
GN_EPS = 64e-5
EXP_NEG_W_OFFSET = math.exp(-0.5)

WKV_CHUNK = 64
HEADS_PER_GROUP = 4
GROUP_W = HEADS_PER_GROUP * B_HEAD_DIM
N_GROUPS = B_HEADS // HEADS_PER_GROUP

ROW_TILE = 512
STREAMS_PER_BLOCK = 4
IN_COL_TILE = 4224
VMEM_LIMIT = 56 * 1024 * 1024

F32 = jnp.float32
BF16 = jnp.bfloat16


def _rms_norm_rows(x, g):
    ms = jnp.mean(x * x, axis=-1, keepdims=True)
    return x * lax.rsqrt(ms + NORM_EPS) * g


def _gelu(x):
    return 0.5 * x * (1.0 + lax.erf(x * (1.0 / math.sqrt(2.0))))


def _silu(x):
    return x * jax.nn.sigmoid(x)


_NN = (((1,), (0,)), ((), ()))
_NT = (((1,), (1,)), ((), ()))
_TN = (((0,), (0,)), ((), ()))


def _split_bf16(x, n):
    parts = []
    for _ in range(n - 1):
        p = x.astype(BF16)
        parts.append(p)
        x = x - p.astype(F32)
    parts.append(x.astype(BF16))
    return parts


def _dg(a, b, dims):
    return lax.dot_general(a, b, dims, preferred_element_type=F32)


def _mm(a, b, dims=_NN):
    return _dg(a.astype(BF16), b.astype(BF16), dims)


def _mm_exact_rhs(a, b, n):
    bb = b.astype(BF16)
    return sum(_dg(p, bb, _NN) for p in _split_bf16(a, n))


def _mm_exact_lhs(a, b, n):
    ab = a.astype(BF16)
    return sum(_dg(ab, p, _NN) for p in _split_bf16(b, n))


def _iota(shape, dim):
    return lax.broadcasted_iota(jnp.int32, shape, dim)


def _gmlp_kernel(x_ref, g_ref, w_ref, lng_ref, lnb_ref, sguw_ref, sgub_ref, outa_ref, *vn_refs,
                 rows, sgu_l):
    h = _rms_norm_rows(x_ref[...], g_ref[...]).astype(BF16)
    v = jnp.dot(h, w_ref[:, D_A:2 * D_A], preferred_element_type=F32)
    u = jnp.dot(h, w_ref[:, 0:D_A], preferred_element_type=F32)
    g_a = jnp.dot(h, w_ref[:, 2 * D_A:3 * D_A], preferred_element_type=F32)
    gv = _gelu(v)
    mean = jnp.mean(gv, axis=-1, keepdims=True)
    cen = gv - mean
    var = jnp.mean(cen * cen, axis=-1, keepdims=True)
    vn = cen * lax.rsqrt(var + LN_EPS) * lng_ref[...] + lnb_ref[...]
    if vn_refs:
        vn_refs[0][...] = vn
    tril_l = _iota((sgu_l, sgu_l), 0) >= _iota((sgu_l, sgu_l), 1)
    vn_b = vn.astype(BF16)
    f_heads = []
    for hd in range(A_HEADS):
        w = jnp.where(tril_l, sguw_ref[hd], 0.0).astype(BF16)
        cols = slice(hd * A_HEAD_DIM, (hd + 1) * A_HEAD_DIM)
        f_chunks = [jnp.dot(w, vn_b[i * sgu_l:(i + 1) * sgu_l, cols], preferred_element_type=F32)
                    for i in range(rows // sgu_l)]
        f_heads.append(jnp.concatenate(f_chunks, axis=0) if len(f_chunks) > 1 else f_chunks[0])
    f = jnp.concatenate(f_heads, axis=1) + sgub_ref[...]
    outa_ref[...] = (_gelu(u) * f * _silu(g_a)).astype(outa_ref.dtype)


def _gmlp(x2d, t, lw, with_vn):
    n_rows = x2d.shape[0]
    sgu_l = min(t, SGU_CHUNK)
    rows = ROW_TILE
    assert n_rows % rows == 0 and rows % sgu_l == 0 and t % sgu_l == 0
    row = lambda p: p.reshape(1, -1)
    sgu_w = lw["sgu_w"][:, :sgu_l, :sgu_l]
    sgu_b = jnp.tile(jnp.repeat(lw["sgu_b"][:, :sgu_l].T, A_HEAD_DIM, axis=1), (rows // sgu_l, 1))
    const = lambda shape: pl.BlockSpec(shape, lambda i: (0,) * len(shape))
    out_specs = [pl.BlockSpec((rows, D_A), lambda i: (i, 0))]
    out_shape = [jax.ShapeDtypeStruct((n_rows, D_A), BF16)]
    if with_vn:
        out_specs.append(pl.BlockSpec((rows, D_A), lambda i: (i, 0)))
        out_shape.append(jax.ShapeDtypeStruct((n_rows, D_A), F32))
    outs = pl.pallas_call(
        functools.partial(_gmlp_kernel, rows=rows, sgu_l=sgu_l),
        grid=(n_rows // rows,),
        in_specs=[
            pl.BlockSpec((rows, D_MODEL), lambda i: (i, 0)),
            const((1, D_MODEL)),
            pl.BlockSpec((D_MODEL, D_PROJ_A), lambda i: (0, 0), pipeline_mode=pl.Buffered(1)),
            const((1, D_A)), const((1, D_A)),
            const((A_HEADS, sgu_l, sgu_l)), const((rows, D_A)),
        ],
        out_specs=out_specs,
        out_shape=out_shape,
        compiler_params=pltpu.CompilerParams(
            dimension_semantics=("arbitrary",), vmem_limit_bytes=VMEM_LIMIT),
        name="gmlp",
    )(x2d, row(lw["norm_g"]), lw["w_in_a"], row(lw["sgu_ln_g"]), row(lw["sgu_ln_b"]), sgu_w, sgu_b)
    return outs[0], (outs[1] if with_vn else None)


def _proj_in_kernel(x_ref, g_ref, w_ref, z_ref):
    h = _rms_norm_rows(x_ref[...], g_ref[...])
    z_ref[...] = jnp.dot(h.astype(BF16), w_ref[...], preferred_element_type=F32)


def _proj_in(x2d, norm_g, w_in_b):
    rows = x2d.shape[0]
    assert rows % ROW_TILE == 0 and D_PROJ_B % IN_COL_TILE == 0
    return pl.pallas_call(
        _proj_in_kernel,
        grid=(D_PROJ_B // IN_COL_TILE, rows // ROW_TILE),
        in_specs=[
            pl.BlockSpec((ROW_TILE, D_MODEL), lambda j, i: (i, 0)),
            pl.BlockSpec((1, D_MODEL), lambda j, i: (0, 0)),
            pl.BlockSpec((D_MODEL, IN_COL_TILE), lambda j, i: (0, j), pipeline_mode=pl.Buffered(1)),
        ],
        out_specs=pl.BlockSpec((ROW_TILE, IN_COL_TILE), lambda j, i: (i, j)),
        out_shape=jax.ShapeDtypeStruct((rows, D_PROJ_B), F32),
        compiler_params=pltpu.CompilerParams(
            dimension_semantics=("arbitrary", "arbitrary"), vmem_limit_bytes=VMEM_LIMIT),
        name="proj_in",
    )(x2d, norm_g.reshape(1, D_MODEL), w_in_b)


def _run_interleaved(*gens):
    live = list(gens)
    while live:
        for gen in list(live):
            try:
                next(gen)
            except StopIteration:
                live.remove(gen)


def _mixer_kernel(z_ref, outa_ref, x_ref, shift0_ref, wkv0_ref, mu_ref, wl_ref, bl_ref,
                  kk_ref, ka_ref, rk_ref, gng_ref, gnb_ref, wout_ref, fg_ref,
                  y_ref, shift_out_ref, wkv_out_ref, s_scr, prev_scr, *, n_streams, final_norm):
    c = WKV_CHUNK
    gw = GROUP_W
    tb = n_streams * c
    t_idx = pl.program_id(1)
    n_t = pl.num_programs(1)

    bd_mask = _iota((gw, gw), 0) // B_HEAD_DIM == _iota((gw, gw), 1) // B_HEAD_DIM
    group_ones = bd_mask.astype(F32)

    @pl.when(t_idx == 0)
    def _init():
        prev_scr[...] = shift0_ref[...]
        tile_k = (_iota((B_HEAD_DIM, gw), 0) == _iota((B_HEAD_DIM, gw), 1) % B_HEAD_DIM).astype(F32)
        for si in range(n_streams):
            for g in range(N_GROUPS):
                s0 = wkv0_ref[si, g * gw:(g + 1) * gw, :]
                s_scr[si * N_GROUPS + g] = jnp.where(bd_mask, _mm_exact_rhs(s0, tile_k, 3), 0.0)

    def gsum(x):
        return jnp.concatenate(
            [_mm(x[:, g * gw:(g + 1) * gw], group_ones) for g in range(N_GROUPS)], axis=1)

    t_w = _iota((c, gw), 0)
    s_w = _iota((c, gw), 1) % c
    strict_w = t_w > s_w
    incl_w = t_w >= s_w
    eye_w = (t_w == s_w).astype(F32)
    i_bd = _iota((gw, gw), 0) % c
    j_bd = _iota((gw, gw), 1) % c
    head_w = _iota((c, gw), 1) // B_HEAD_DIM
    first_step = _iota((c, N_SHIFT), 0) == 0
    lanes_of = lambda g: slice(g * gw, (g + 1) * gw)

    def block_diag(x):
        return jnp.where(bd_mask, jnp.concatenate([x] * HEADS_PER_GROUP, axis=0), 0.0)

    pair_w = (t_w % 2 == 1) & (s_w == t_w - 1)
    off_masks = {}
    m_ = 2
    while m_ < c:
        off_masks[m_] = bd_mask & (i_bd // (2 * m_) == j_bd // (2 * m_)) \
            & ((i_bd // m_) % 2 == 1) & ((j_bd // m_) % 2 == 0)
        m_ *= 2
    eye_b = eye_w.astype(BF16)

    ops = {}
    pre = {}
    y_of = {}

    def prep(streams):
        nr = len(streams) * c
        parts = []
        for si in streams:
            zs = z_ref[si, :, 0:N_SHIFT]
            last_row = zs[c - 1:c, :]
            z_prev = jnp.where(first_step, prev_scr[si], pltpu.roll(zs, 1, axis=0))
            parts.append(zs + mu_ref[...] * (z_prev - zs))
            prev_scr[si] = last_row
            shift_out_ref[si] = last_row
            yield
        zmix = jnp.concatenate(parts, axis=0) if len(parts) > 1 else parts[0]
        r = zmix[:, 0:D_B]
        k = zmix[:, D_B:2 * D_B]
        vb = zmix[:, 2 * D_B:3 * D_B]
        lo = zmix[:, 3 * D_B:3 * D_B + 2 * LORA]
        lo_in = jnp.where(_iota((nr, 2 * LORA), 1) < LORA, jnp.tanh(lo), lo)
        da = _mm(lo_in, wl_ref[...]) + bl_ref[...]
        yield
        logw = -EXP_NEG_W_OFFSET * jax.nn.sigmoid(da[:, 0:D_B])
        a = jax.nn.sigmoid(da[:, D_B:2 * D_B])
        yield
        kk = k * kk_ref[...]
        norm2 = gsum(kk * kk)
        yield
        kk = kk / jnp.maximum(jnp.sqrt(norm2), 1e-12)
        k = k * (1.0 + (a - 1.0) * ka_ref[...])
        b = kk * a
        yield
        tril = _iota((nr, nr), 0) >= _iota((nr, nr), 1)
        same_chunk = _iota((nr, nr), 0) // c == _iota((nr, nr), 1) // c
        cum = _mm_exact_lhs(jnp.where(tril & same_chunk, 1.0, 0.0), logw, 2)
        bonus = gsum(r * k * rk_ref[...]) * vb
        yield
        for j, si in enumerate(streams):
            rows = slice(j * c, (j + 1) * c)
            cum_c = cum[rows]
            cum_last = cum_c[c - 1:c, :]
            g_inv = jnp.exp(-cum_c)
            g_hat = jnp.exp(cum_last - cum_c)
            ops[si] = dict(
                r_t=r[rows] * jnp.exp(cum_c),
                kk_t=kk[rows] * jnp.exp(cum_c - logw[rows]),
                b_t=b[rows] * g_inv, k_t=k[rows] * g_inv,
                b_h=b[rows] * g_hat, k_h=k[rows] * g_hat,
                v=vb[rows], g_last=jnp.exp(cum_last), bonus=bonus[rows])
            yield

    def chains(streams):
        keys = [(si, g) for si in streams for g in range(N_GROUPS)]
        a_b, a_k, a_rb, a_rk, v_bd, l_bd, d_w = {}, {}, {}, {}, {}, {}, {}
        for ch in keys:
            si, g = ch
            lanes = lanes_of(g)
            o = ops[si]
            lhs = jnp.concatenate([o["kk_t"][:, lanes], o["r_t"][:, lanes]], axis=0)
            rb = jnp.concatenate(
                [jnp.where(head_w == h, o["b_t"][:, lanes], 0.0) for h in range(HEADS_PER_GROUP)]
                + [jnp.where(head_w == h, o["k_t"][:, lanes], 0.0) for h in range(HEADS_PER_GROUP)],
                axis=0)
            a_all = _mm(lhs, rb, _NT)
            a_b[ch], a_k[ch] = a_all[:c, :gw], a_all[:c, gw:]
            a_rb[ch], a_rk[ch] = a_all[c:, :gw], a_all[c:, gw:]
        yield
        for ch in keys:
            si, g = ch
            l_w = jnp.where(strict_w, a_b[ch], 0.0).astype(BF16)
            l_bd[ch] = jnp.concatenate([l_w] * HEADS_PER_GROUP, axis=0)
            v_bd[ch] = block_diag(ops[si]["v"][:, lanes_of(g)].astype(BF16))
            d_w[ch] = jnp.where(pair_w, -l_w, eye_b)
        yield
        m = 2
        while m < c:
            tmp = {ch: _mm(d_w[ch], jnp.where(off_masks[m], l_bd[ch], 0.0)).astype(BF16) for ch in keys}
            yield
            d_w = {ch: (d_w[ch].astype(F32) - _mm(tmp[ch], block_diag(d_w[ch]))).astype(BF16) for ch in keys}
            yield
            m *= 2
        av = {ch: _mm(jnp.concatenate([jnp.where(strict_w, a_k[ch], 0.0), jnp.where(incl_w, a_rk[ch], 0.0)],
                                      axis=0), v_bd[ch]) for ch in keys}
        yield
        du = {ch: _mm(d_w[ch], jnp.concatenate(
            [block_diag(ops[ch[0]]["kk_t"][:, lanes_of(ch[1])].astype(BF16)), block_diag(av[ch][:c].astype(BF16))],
            axis=1)) for ch in keys}
        for ch in keys:
            pre[ch] = dict(dk=du[ch][:, :gw], u0=-du[ch][:, gw:], yv=av[ch][c:], a_rb=a_rb[ch])
        yield

    def state_dep(streams):
        for si in streams:
            o = ops[si]
            s_old = [s_scr[si * N_GROUPS + g] for g in range(N_GROUPS)]
            pu = [_mm(jnp.concatenate([pre[(si, g)]["dk"], o["r_t"][:, lanes_of(g)]], axis=0), s_old[g], _NT)
                  for g in range(N_GROUPS)]
            u = [pre[(si, g)]["u0"] - pu[g][:c] for g in range(N_GROUPS)]
            yield
            upd = [_mm(jnp.concatenate([u[g], o["v"][:, lanes_of(g)]], axis=0),
                       jnp.concatenate([o["b_h"][:, lanes_of(g)], o["k_h"][:, lanes_of(g)]], axis=0), _TN)
                   for g in range(N_GROUPS)]
            for g in range(N_GROUPS):
                s_scr[si * N_GROUPS + g] = (s_old[g] * o["g_last"][:, lanes_of(g)]
                                            + jnp.where(bd_mask, upd[g], 0.0))
            yield
            y_groups = [pu[g][c:] + pre[(si, g)]["yv"]
                        + _mm(jnp.where(incl_w, pre[(si, g)]["a_rb"], 0.0), block_diag(u[g]))
                        for g in range(N_GROUPS)]
            y_of[si] = jnp.concatenate(y_groups, axis=1)
            yield

    half = max(n_streams // 2, 1)
    first, second = list(range(half)), list(range(half, n_streams))
    _run_interleaved(prep(first))
    o_a = jnp.dot(outa_ref[...].reshape(tb, D_A), wout_ref[0:D_A, :], preferred_element_type=F32)
    _run_interleaved(chains(first), prep(second))
    _run_interleaved(state_dep(first), chains(second))
    _run_interleaved(state_dep(second))

    y = jnp.concatenate([y_of[si] for si in range(n_streams)], axis=0) if n_streams > 1 else y_of[0]
    bonus = (jnp.concatenate([ops[si]["bonus"] for si in range(n_streams)], axis=0)
             if n_streams > 1 else ops[0]["bonus"])
    mu_y = gsum(y) * (1.0 / B_HEAD_DIM)
    y_cen = y - mu_y
    var_y = gsum(y_cen * y_cen) * (1.0 / B_HEAD_DIM)
    y = y_cen * lax.rsqrt(var_y + GN_EPS) * gng_ref[...] + gnb_ref[...]
    y = y + bonus
    g_b = z_ref[:, :, N_SHIFT:D_PROJ_B].reshape(tb, D_B)
    out_b = (y * _silu(g_b)).astype(BF16)

    out = (x_ref[...].reshape(tb, D_MODEL) + o_a
           + jnp.dot(out_b, wout_ref[D_A:D_A + D_B, :], preferred_element_type=F32))
    if final_norm:
        out = _rms_norm_rows(out, fg_ref[...])
    y_ref[...] = out.reshape(n_streams, c, D_MODEL)

    @pl.when(t_idx == n_t - 1)
    def _final():
        untile_k = (_iota((gw, B_HEAD_DIM), 0) % B_HEAD_DIM == _iota((gw, B_HEAD_DIM), 1)).astype(F32)
        for si in range(n_streams):
            for g in range(N_GROUPS):
                wkv_out_ref[si, g * gw:(g + 1) * gw, :] = _mm_exact_rhs(
                    s_scr[si * N_GROUPS + g], untile_k, 3)


def _mixer(z, out_a, x, shift0, wkv0, lw, final_g, final_norm):
    bsz, t, _ = z.shape
    c = WKV_CHUNK
    ns = STREAMS_PER_BLOCK
    assert t % c == 0 and bsz % ns == 0
    row = lambda p: p.reshape(1, -1)
    zeros = jnp.zeros((LORA, D_B), F32)
    w_lora = jnp.concatenate([jnp.concatenate([lw["w2"], zeros], axis=1),
                              jnp.concatenate([zeros, lw["a2"]], axis=1)], axis=0)
    b_lora = jnp.concatenate([lw["w0"], lw["a0"]]).reshape(1, 2 * D_B)
    const = lambda shape: pl.BlockSpec(shape, lambda b_, t_: (0,) * len(shape))
    in_specs = [
        pl.BlockSpec((ns, c, D_PROJ_B), lambda b_, t_: (b_, t_, 0)),
        pl.BlockSpec((ns, c, D_A), lambda b_, t_: (b_, t_, 0)),
        pl.BlockSpec((ns, c, D_MODEL), lambda b_, t_: (b_, t_, 0)),
        pl.BlockSpec((ns, 1, N_SHIFT), lambda b_, t_: (b_, 0, 0)),
        pl.BlockSpec((ns, D_B, B_HEAD_DIM), lambda b_, t_: (b_, 0, 0)),
        const((1, N_SHIFT)),
        const((2 * LORA, 2 * D_B)), const((1, 2 * D_B)),
        const((1, D_B)), const((1, D_B)), const((1, D_B)), const((1, D_B)), const((1, D_B)),
        pl.BlockSpec((D_A + D_B, D_MODEL), lambda b_, t_: (0, 0), pipeline_mode=pl.Buffered(1)),
        const((1, D_MODEL)),
    ]
    out_specs = [
        pl.BlockSpec((ns, c, D_MODEL), lambda b_, t_: (b_, t_, 0)),
        pl.BlockSpec((ns, 1, N_SHIFT), lambda b_, t_: (b_, 0, 0)),
        pl.BlockSpec((ns, D_B, B_HEAD_DIM), lambda b_, t_: (b_, 0, 0)),
    ]
    out_shape = [
        jax.ShapeDtypeStruct((bsz, t, D_MODEL), F32),
        jax.ShapeDtypeStruct((bsz, 1, N_SHIFT), F32),
        jax.ShapeDtypeStruct((bsz, D_B, B_HEAD_DIM), F32),
    ]
    y, new_shift, wkv = pl.pallas_call(
        functools.partial(_mixer_kernel, n_streams=ns, final_norm=final_norm),
        grid=(bsz // ns, t // c),
        in_specs=in_specs,
        out_specs=out_specs,
        out_shape=out_shape,
        scratch_shapes=[pltpu.VMEM((ns * N_GROUPS, GROUP_W, GROUP_W), F32),
                        pltpu.VMEM((ns, 1, N_SHIFT), F32)],
        compiler_params=pltpu.CompilerParams(
            dimension_semantics=("arbitrary", "arbitrary"), vmem_limit_bytes=VMEM_LIMIT),
        name="mixer",
    )(z, out_a, x, shift0, wkv0.reshape(bsz, D_B, B_HEAD_DIM),
      row(lw["shift_mu"]), w_lora, b_lora,
      row(lw["k_k"]), row(lw["k_a"]), row(lw["r_k"]), row(lw["gn_g"]), row(lw["gn_b"]),
      lw["w_out"], row(final_g))
    return y, new_shift, wkv.reshape(bsz, B_HEADS, B_HEAD_DIM, B_HEAD_DIM)


def _hybrid_layer(x, wkv0, shift0, lw, final_g, final_norm, with_vn):
    bsz, t, _ = x.shape
    x2d = x.reshape(bsz * t, D_MODEL)
    out_a, vn = _gmlp(x2d, t, lw, with_vn)
    z_b = _proj_in(x2d, lw["norm_g"], lw["w_in_b"])
    y, new_shift, wkv = _mixer(z_b.reshape(bsz, t, D_PROJ_B), out_a.reshape(bsz, t, D_A), x,
                               shift0, wkv0, lw, final_g, final_norm)
    if with_vn:
        vn = vn.reshape(bsz, t, D_A)
    return y, wkv, new_shift, vn


_LAYER_PARAMS = ("norm_g", "w_in", "w_out", "sgu_ln_g", "sgu_ln_b", "sgu_w", "sgu_b", "shift_mu",
                 "w0", "w2", "a0", "a2", "k_k", "k_a", "r_k", "gn_g", "gn_b")


def kernel(x_prompt, x_sample, state_b_wkv, state_b_shift, norm_g, w_in, w_out, sgu_ln_g, sgu_ln_b,
           sgu_w, sgu_b, shift_mu, w0, w2, a0, a2, k_k, k_a, r_k, gn_g, gn_b, final_g):
    stacked = dict(zip(_LAYER_PARAMS, (norm_g, w_in, w_out, sgu_ln_g, sgu_ln_b, sgu_w, sgu_b, shift_mu,
                                       w0, w2, a0, a2, k_k, k_a, r_k, gn_g, gn_b)))
    depth = w_in.shape[0]
    bp = x_prompt.shape[0]
    zero_wkv = jnp.zeros((bp, B_HEADS, B_HEAD_DIM, B_HEAD_DIM), F32)
    zero_shift = jnp.zeros((bp, 1, N_SHIFT), F32)
    yp, ys = x_prompt, x_sample
    wkv_p, shift_p, wkv_s, shift_s, sgu_v_s = [], [], [], [], []
    for l in range(depth):
        lw = {name: p[l] for name, p in stacked.items()}
        lw["w_in_a"] = lw["w_in"][:, :D_PROJ_A].astype(BF16)
        lw["w_in_b"] = lw["w_in"][:, D_PROJ_A:].astype(BF16)
        lw["w_out"] = lw["w_out"].astype(BF16)
        last = l == depth - 1
        yp, s_p, sh_p, _ = _hybrid_layer(yp, zero_wkv, zero_shift, lw, final_g, last, False)
        ys, s_s, sh_s, vn_s = _hybrid_layer(ys, state_b_wkv[l], state_b_shift[l], lw, final_g, last, True)
        wkv_p.append(s_p); shift_p.append(sh_p)
        wkv_s.append(s_s); shift_s.append(sh_s); sgu_v_s.append(vn_s)
    return (yp, ys, jnp.stack(wkv_p), jnp.stack(shift_p), jnp.stack(wkv_s), jnp.stack(shift_s),
            jnp.stack(sgu_v_s))
```

```python
import functools
import math

import jax
import jax.numpy as jnp
from jax import lax
from jax.experimental import pallas as pl
from jax.experimental.pallas import tpu as pltpu

D_MODEL = 2048
D_A = 1024
D_B = 1024
SGU_CHUNK = 128
A_HEADS = 8
A_HEAD_DIM = D_A // A_HEADS
B_HEAD_DIM = 64
B_HEADS = D_B // B_HEAD_DIM
LORA = 64
N_SHIFT = 3 * D_B + 2 * LORA
D_PROJ_A = 3 * D_A
D_PROJ_B = N_SHIFT + D_B
NORM_EPS = 1e-6
LN_EPS = 1e-5
GN_EPS = 64e-5
EXP_NEG_W_OFFSET = math.exp(-0.5)

WKV_CHUNK = 64
HEADS_PER_GROUP = 4
GROUP_W = HEADS_PER_GROUP * B_HEAD_DIM
N_GROUPS = B_HEADS // HEADS_PER_GROUP

ROW_TILE = 512
STREAMS_PER_BLOCK = 4
IN_COL_TILE = 4224
VMEM_LIMIT = 56 * 1024 * 1024

F32 = jnp.float32
BF16 = jnp.bfloat16


def _rms_norm_rows(x, g):
    ms = jnp.mean(x * x, axis=-1, keepdims=True)
    return x * lax.rsqrt(ms + NORM_EPS) * g


def _gelu(x):
    return 0.5 * x * (1.0 + lax.erf(x * (1.0 / math.sqrt(2.0))))


def _silu(x):
    return x * jax.nn.sigmoid(x)


_NN = (((1,), (0,)), ((), ()))
_NT = (((1,), (1,)), ((), ()))
_TN = (((0,), (0,)), ((), ()))


def _split_bf16(x, n):
    parts = []
    for _ in range(n - 1):
        p = x.astype(BF16)
        parts.append(p)
        x = x - p.astype(F32)
    parts.append(x.astype(BF16))
    return parts


def _dg(a, b, dims):
    return lax.dot_general(a, b, dims, preferred_element_type=F32)


def _mm(a, b, dims=_NN):
    return _dg(a.astype(BF16), b.astype(BF16), dims)


def _mm_exact_lhs(a, b, n):
    ab = a.astype(BF16)
    return sum(_dg(ab, p, _NN) for p in _split_bf16(b, n))


def _iota(shape, dim):
    return lax.broadcasted_iota(jnp.int32, shape, dim)


def _gmlp_kernel(x_ref, g_ref, w_ref, lng_ref, lnb_ref, sguw_ref, sgub_ref, outa_ref, *vn_refs,
                 rows, sgu_l):
    h = _rms_norm_rows(x_ref[...], g_ref[...]).astype(BF16)
    v = jnp.dot(h, w_ref[:, D_A:2 * D_A], preferred_element_type=F32)
    u = jnp.dot(h, w_ref[:, 0:D_A], preferred_element_type=F32)
    g_a = jnp.dot(h, w_ref[:, 2 * D_A:3 * D_A], preferred_element_type=F32)
    gv = _gelu(v)
    mean = jnp.mean(gv, axis=-1, keepdims=True)
    cen = gv - mean
    var = jnp.mean(cen * cen, axis=-1, keepdims=True)
    vn = cen * lax.rsqrt(var + LN_EPS) * lng_ref[...] + lnb_ref[...]
    if vn_refs:
        vn_refs[0][...] = vn
    tril_l = _iota((sgu_l, sgu_l), 0) >= _iota((sgu_l, sgu_l), 1)
    vn_b = vn.astype(BF16)
    f_heads = []
    for hd in range(A_HEADS):
        w = jnp.where(tril_l, sguw_ref[hd], 0.0).astype(BF16)
        cols = slice(hd * A_HEAD_DIM, (hd + 1) * A_HEAD_DIM)
        f_chunks = [jnp.dot(w, vn_b[i * sgu_l:(i + 1) * sgu_l, cols], preferred_element_type=F32)
                    for i in range(rows // sgu_l)]
        f_heads.append(jnp.concatenate(f_chunks, axis=0) if len(f_chunks) > 1 else f_chunks[0])
    f = jnp.concatenate(f_heads, axis=1) + sgub_ref[...]
    outa_ref[...] = (_gelu(u) * f * _silu(g_a)).astype(outa_ref.dtype)


def _gmlp(x2d, t, lw, with_vn):
    n_rows = x2d.shape[0]
    sgu_l = min(t, SGU_CHUNK)
    rows = ROW_TILE
    assert n_rows % rows == 0 and rows % sgu_l == 0 and t % sgu_l == 0
    row = lambda p: p.reshape(1, -1)
    sgu_w = lw["sgu_w"][:, :sgu_l, :sgu_l]
    sgu_b = jnp.tile(jnp.repeat(lw["sgu_b"][:, :sgu_l].T, A_HEAD_DIM, axis=1), (rows // sgu_l, 1))
    const = lambda shape: pl.BlockSpec(shape, lambda i: (0,) * len(shape))
    out_specs = [pl.BlockSpec((rows, D_A), lambda i: (i, 0))]
    out_shape = [jax.ShapeDtypeStruct((n_rows, D_A), BF16)]
    if with_vn:
        out_specs.append(pl.BlockSpec((rows, D_A), lambda i: (i, 0)))
        out_shape.append(jax.ShapeDtypeStruct((n_rows, D_A), F32))
    outs = pl.pallas_call(
        functools.partial(_gmlp_kernel, rows=rows, sgu_l=sgu_l),
        grid=(n_rows // rows,),
        in_specs=[
            pl.BlockSpec((rows, D_MODEL), lambda i: (i, 0)),
            const((1, D_MODEL)),
            pl.BlockSpec((D_MODEL, D_PROJ_A), lambda i: (0, 0), pipeline_mode=pl.Buffered(1)),
            const((1, D_A)), const((1, D_A)),
            const((A_HEADS, sgu_l, sgu_l)), const((rows, D_A)),
        ],
        out_specs=out_specs,
        out_shape=out_shape,
        compiler_params=pltpu.CompilerParams(
            dimension_semantics=("arbitrary",), vmem_limit_bytes=VMEM_LIMIT),
        name="gmlp",
    )(x2d, row(lw["norm_g"]), lw["w_in_a"], row(lw["sgu_ln_g"]), row(lw["sgu_ln_b"]), sgu_w, sgu_b)
    return outs[0], (outs[1] if with_vn else None)


def _proj_in_kernel(x_ref, g_ref, w_ref, z_ref):
    h = _rms_norm_rows(x_ref[...], g_ref[...])
    z_ref[...] = jnp.dot(h.astype(BF16), w_ref[...], preferred_element_type=F32)


def _proj_in(x2d, norm_g, w_in_b):
    rows = x2d.shape[0]
    assert rows % ROW_TILE == 0 and D_PROJ_B % IN_COL_TILE == 0
    return pl.pallas_call(
        _proj_in_kernel,
        grid=(D_PROJ_B // IN_COL_TILE, rows // ROW_TILE),
        in_specs=[
            pl.BlockSpec((ROW_TILE, D_MODEL), lambda j, i: (i, 0)),
            pl.BlockSpec((1, D_MODEL), lambda j, i: (0, 0)),
            pl.BlockSpec((D_MODEL, IN_COL_TILE), lambda j, i: (0, j), pipeline_mode=pl.Buffered(1)),
        ],
        out_specs=pl.BlockSpec((ROW_TILE, IN_COL_TILE), lambda j, i: (i, j)),
        out_shape=jax.ShapeDtypeStruct((rows, D_PROJ_B), F32),
        compiler_params=pltpu.CompilerParams(
            dimension_semantics=("arbitrary", "arbitrary"), vmem_limit_bytes=VMEM_LIMIT),
        name="proj_in",
    )(x2d, norm_g.reshape(1, D_MODEL), w_in_b)


def _run_interleaved(*gens):
    live = list(gens)
    while live:
        for gen in list(live):
            try:
                next(gen)
            except StopIteration:
                live.remove(gen)


def _mixer_kernel(z_ref, outa_ref, x_ref, shift0_ref, wkv0_ref, mu_ref, wl_ref, bl_ref,
                  kk_ref, ka_ref, rk_ref, gng_ref, gnb_ref, wout_ref, fg_ref,
                  y_ref, shift_out_ref, wkv_out_ref, s_scr, prev_scr, *, n_streams, final_norm):
    c = WKV_CHUNK
    gw = GROUP_W
    tb = n_streams * c
    t_idx = pl.program_id(1)
    n_t = pl.num_programs(1)

    bd_mask = _iota((gw, gw), 0) // B_HEAD_DIM == _iota((gw, gw), 1) // B_HEAD_DIM
    group_ones = bd_mask.astype(F32)

    @pl.when(t_idx == 0)
    def _init():
        prev_scr[...] = shift0_ref[...]
        for si in range(n_streams):
            for g in range(N_GROUPS):
                s0 = wkv0_ref[si, g * gw:(g + 1) * gw, :]
                s_scr[si * N_GROUPS + g] = jnp.where(
                    bd_mask, jnp.concatenate([s0] * HEADS_PER_GROUP, axis=1), 0.0)

    def gsum(x):
        n = x.shape[0]
        stacked = jnp.concatenate([x[:, g * gw:(g + 1) * gw] for g in range(N_GROUPS)], axis=0)
        sums = _mm(stacked, group_ones)
        return jnp.concatenate([sums[g * n:(g + 1) * n] for g in range(N_GROUPS)], axis=1)

    t_w = _iota((c, gw), 0)
    s_w = _iota((c, gw), 1) % c
    strict_w = t_w > s_w
    incl_w = t_w >= s_w
    eye_w = (t_w == s_w).astype(F32)
    i_bd = _iota((gw, gw), 0) % c
    j_bd = _iota((gw, gw), 1) % c
    head_w = _iota((c, gw), 1) // B_HEAD_DIM
    first_step = _iota((c, N_SHIFT), 0) == 0
    lanes_of = lambda g: slice(g * gw, (g + 1) * gw)

    def block_diag(x):
        return jnp.where(bd_mask, jnp.concatenate([x] * HEADS_PER_GROUP, axis=0), 0.0)

    pair_w = (t_w % 2 == 1) & (s_w == t_w - 1)
    off_masks = {}
    m_ = 2
    while m_ < c:
        off_masks[m_] = bd_mask & (i_bd // (2 * m_) == j_bd // (2 * m_)) \
            & ((i_bd // m_) % 2 == 1) & ((j_bd // m_) % 2 == 0)
        m_ *= 2
    eye_b = eye_w.astype(BF16)

    ops = {}
    pre = {}
    y_of = {}

    def prep(streams):
        nr = len(streams) * c
        parts = []
        for si in streams:
            zs = z_ref[si, :, 0:N_SHIFT]
            last_row = zs[c - 1:c, :]
            z_prev = jnp.where(first_step, prev_scr[si], pltpu.roll(zs, 1, axis=0))
            parts.append(zs + mu_ref[...] * (z_prev - zs))
            prev_scr[si] = last_row
            shift_out_ref[si] = last_row
            yield
        zmix = jnp.concatenate(parts, axis=0) if len(parts) > 1 else parts[0]
        r = zmix[:, 0:D_B]
        k = zmix[:, D_B:2 * D_B]
        vb = zmix[:, 2 * D_B:3 * D_B]
        lo = zmix[:, 3 * D_B:3 * D_B + 2 * LORA]
        lo_in = jnp.where(_iota((nr, 2 * LORA), 1) < LORA, jnp.tanh(lo), lo)
        da = _mm(lo_in, wl_ref[...]) + bl_ref[...]
        yield
        logw = -EXP_NEG_W_OFFSET * jax.nn.sigmoid(da[:, 0:D_B])
        a = jax.nn.sigmoid(da[:, D_B:2 * D_B])
        yield
        kk = k * kk_ref[...]
        norm2 = gsum(kk * kk)
        yield
        kk = kk / jnp.maximum(jnp.sqrt(norm2), 1e-12)
        k = k * (1.0 + (a - 1.0) * ka_ref[...])
        b = kk * a
        yield
        tril = _iota((nr, nr), 0) >= _iota((nr, nr), 1)
        same_chunk = _iota((nr, nr), 0) // c == _iota((nr, nr), 1) // c
        cum = _mm_exact_lhs(jnp.where(tril & same_chunk, 1.0, 0.0), logw, 2)
        bonus = gsum(r * k * rk_ref[...]) * vb
        yield
        for j, si in enumerate(streams):
            rows = slice(j * c, (j + 1) * c)
            cum_c = cum[rows]
            cum_last = cum_c[c - 1:c, :]
            g_inv = jnp.exp(-cum_c)
            g_hat = jnp.exp(cum_last - cum_c)
            ops[si] = dict(
                r_t=r[rows] * jnp.exp(cum_c),
                kk_t=kk[rows] * jnp.exp(cum_c - logw[rows]),
                b_t=b[rows] * g_inv, k_t=k[rows] * g_inv,
                b_h=b[rows] * g_hat, k_h=k[rows] * g_hat,
                v=vb[rows], g_last=jnp.exp(cum_last), bonus=bonus[rows])
            yield

    def chains(streams):
        keys = [(si, g) for si in streams for g in range(N_GROUPS)]
        a_b, a_k, a_rb, a_rk, v_bd, l_bd, d_w = {}, {}, {}, {}, {}, {}, {}
        for ch in keys:
            si, g = ch
            lanes = lanes_of(g)
            o = ops[si]
            lhs = jnp.concatenate([o["kk_t"][:, lanes], o["r_t"][:, lanes]], axis=0)
            rb = jnp.concatenate(
                [jnp.where(head_w == h, o["b_t"][:, lanes], 0.0) for h in range(HEADS_PER_GROUP)]
                + [jnp.where(head_w == h, o["k_t"][:, lanes], 0.0) for h in range(HEADS_PER_GROUP)],
                axis=0)
            a_all = _mm(lhs, rb, _NT)
            a_b[ch], a_k[ch] = a_all[:c, :gw], a_all[:c, gw:]
            a_rb[ch], a_rk[ch] = a_all[c:, :gw], a_all[c:, gw:]
        yield
        for ch in keys:
            si, g = ch
            l_w = jnp.where(strict_w, a_b[ch], 0.0).astype(BF16)
            l_bd[ch] = jnp.concatenate([l_w] * HEADS_PER_GROUP, axis=0)
            v_bd[ch] = block_diag(ops[si]["v"][:, lanes_of(g)].astype(BF16))
            d_w[ch] = jnp.where(pair_w, -l_w, eye_b)
        yield
        m = 2
        while m < c:
            tmp = {ch: _mm(d_w[ch], jnp.where(off_masks[m], l_bd[ch], 0.0)).astype(BF16) for ch in keys}
            yield
            d_w = {ch: (d_w[ch].astype(F32) - _mm(tmp[ch], block_diag(d_w[ch]))).astype(BF16) for ch in keys}
            yield
            m *= 2
        av = {ch: _mm(jnp.concatenate([jnp.where(strict_w, a_k[ch], 0.0), jnp.where(incl_w, a_rk[ch], 0.0)],
                                      axis=0), v_bd[ch]) for ch in keys}
        yield
        du = {ch: _mm(d_w[ch], jnp.concatenate(
            [block_diag(ops[ch[0]]["kk_t"][:, lanes_of(ch[1])].astype(BF16)), block_diag(av[ch][:c].astype(BF16))],
            axis=1)) for ch in keys}
        for ch in keys:
            pre[ch] = dict(dk=du[ch][:, :gw], u0=-du[ch][:, gw:], yv=av[ch][c:], a_rb=a_rb[ch])
        yield

    def state_dep(streams):
        for si in streams:
            o = ops[si]
            s_old = [s_scr[si * N_GROUPS + g] for g in range(N_GROUPS)]
            pu = [_mm(jnp.concatenate([pre[(si, g)]["dk"], o["r_t"][:, lanes_of(g)]], axis=0), s_old[g], _NT)
                  for g in range(N_GROUPS)]
            u = [pre[(si, g)]["u0"] - pu[g][:c] for g in range(N_GROUPS)]
            yield
            upd = [_mm(jnp.concatenate([u[g], o["v"][:, lanes_of(g)]], axis=0),
                       jnp.concatenate([o["b_h"][:, lanes_of(g)], o["k_h"][:, lanes_of(g)]], axis=0), _TN)
                   for g in range(N_GROUPS)]
            for g in range(N_GROUPS):
                s_scr[si * N_GROUPS + g] = (s_old[g] * o["g_last"][:, lanes_of(g)]
                                            + jnp.where(bd_mask, upd[g], 0.0))
            yield
            y_groups = [pu[g][c:] + pre[(si, g)]["yv"]
                        + _mm(jnp.where(incl_w, pre[(si, g)]["a_rb"], 0.0), block_diag(u[g]))
                        for g in range(N_GROUPS)]
            y_of[si] = jnp.concatenate(y_groups, axis=1)
            yield

    half = max(n_streams // 2, 1)
    first, second = list(range(half)), list(range(half, n_streams))
    _run_interleaved(prep(first))
    o_a = jnp.dot(outa_ref[...].reshape(tb, D_A), wout_ref[0:D_A, :], preferred_element_type=F32)
    _run_interleaved(chains(first), prep(second))
    _run_interleaved(state_dep(first), chains(second))
    _run_interleaved(state_dep(second))

    y = jnp.concatenate([y_of[si] for si in range(n_streams)], axis=0) if n_streams > 1 else y_of[0]
    bonus = (jnp.concatenate([ops[si]["bonus"] for si in range(n_streams)], axis=0)
             if n_streams > 1 else ops[0]["bonus"])
    mu_y = gsum(y) * (1.0 / B_HEAD_DIM)
    y_cen = y - mu_y
    var_y = gsum(y_cen * y_cen) * (1.0 / B_HEAD_DIM)
    y = y_cen * lax.rsqrt(var_y + GN_EPS) * gng_ref[...] + gnb_ref[...]
    y = y + bonus
    g_b = z_ref[:, :, N_SHIFT:D_PROJ_B].reshape(tb, D_B)
    out_b = (y * _silu(g_b)).astype(BF16)

    out = (x_ref[...].reshape(tb, D_MODEL) + o_a
           + jnp.dot(out_b, wout_ref[D_A:D_A + D_B, :], preferred_element_type=F32))
    if final_norm:
        out = _rms_norm_rows(out, fg_ref[...])
    y_ref[...] = out.reshape(n_streams, c, D_MODEL)

    @pl.when(t_idx == n_t - 1)
    def _final():
        d = B_HEAD_DIM
        for si in range(n_streams):
            for g in range(N_GROUPS):
                s_bd = s_scr[si * N_GROUPS + g]
                wkv_out_ref[si, g * gw:(g + 1) * gw, :] = jnp.concatenate(
                    [s_bd[h * d:(h + 1) * d, h * d:(h + 1) * d] for h in range(HEADS_PER_GROUP)], axis=0)


def _mixer(z, out_a, x, shift0, wkv0, lw, final_g, final_norm):
    bsz, t, _ = z.shape
    c = WKV_CHUNK
    ns = STREAMS_PER_BLOCK
    assert t % c == 0 and bsz % ns == 0
    row = lambda p: p.reshape(1, -1)
    zeros = jnp.zeros((LORA, D_B), F32)
    w_lora = jnp.concatenate([jnp.concatenate([lw["w2"], zeros], axis=1),
                              jnp.concatenate([zeros, lw["a2"]], axis=1)], axis=0)
    b_lora = jnp.concatenate([lw["w0"], lw["a0"]]).reshape(1, 2 * D_B)
    const = lambda shape: pl.BlockSpec(shape, lambda b_, t_: (0,) * len(shape))
    in_specs = [
        pl.BlockSpec((ns, c, D_PROJ_B), lambda b_, t_: (b_, t_, 0)),
        pl.BlockSpec((ns, c, D_A), lambda b_, t_: (b_, t_, 0)),
        pl.BlockSpec((ns, c, D_MODEL), lambda b_, t_: (b_, t_, 0)),
        pl.BlockSpec((ns, 1, N_SHIFT), lambda b_, t_: (b_, 0, 0)),
        pl.BlockSpec((ns, D_B, B_HEAD_DIM), lambda b_, t_: (b_, 0, 0)),
        const((1, N_SHIFT)),
        const((2 * LORA, 2 * D_B)), const((1, 2 * D_B)),
        const((1, D_B)), const((1, D_B)), const((1, D_B)), const((1, D_B)), const((1, D_B)),
        pl.BlockSpec((D_A + D_B, D_MODEL), lambda b_, t_: (0, 0), pipeline_mode=pl.Buffered(1)),
        const((1, D_MODEL)),
    ]
    out_specs = [
        pl.BlockSpec((ns, c, D_MODEL), lambda b_, t_: (b_, t_, 0)),
        pl.BlockSpec((ns, 1, N_SHIFT), lambda b_, t_: (b_, 0, 0)),
        pl.BlockSpec((ns, D_B, B_HEAD_DIM), lambda b_, t_: (b_, 0, 0)),
    ]
    out_shape = [
        jax.ShapeDtypeStruct((bsz, t, D_MODEL), F32),
        jax.ShapeDtypeStruct((bsz, 1, N_SHIFT), F32),
        jax.ShapeDtypeStruct((bsz, D_B, B_HEAD_DIM), F32),
    ]
    y, new_shift, wkv = pl.pallas_call(
        functools.partial(_mixer_kernel, n_streams=ns, final_norm=final_norm),
        grid=(bsz // ns, t // c),
        in_specs=in_specs,
        out_specs=out_specs,
        out_shape=out_shape,
        scratch_shapes=[pltpu.VMEM((ns * N_GROUPS, GROUP_W, GROUP_W), F32),
                        pltpu.VMEM((ns, 1, N_SHIFT), F32)],
        compiler_params=pltpu.CompilerParams(
            dimension_semantics=("arbitrary", "arbitrary"), vmem_limit_bytes=VMEM_LIMIT),
        name="mixer",
    )(z, out_a, x, shift0, wkv0.reshape(bsz, D_B, B_HEAD_DIM),
      row(lw["shift_mu"]), w_lora, b_lora,
      row(lw["k_k"]), row(lw["k_a"]), row(lw["r_k"]), row(lw["gn_g"]), row(lw["gn_b"]),
      lw["w_out"], row(final_g))
    return y, new_shift, wkv.reshape(bsz, B_HEADS, B_HEAD_DIM, B_HEAD_DIM)


def _hybrid_layer(x, wkv0, shift0, lw, final_g, final_norm, with_vn):
    bsz, t, _ = x.shape
    x2d = x.reshape(bsz * t, D_MODEL)
    out_a, vn = _gmlp(x2d, t, lw, with_vn)
    z_b = _proj_in(x2d, lw["norm_g"], lw["w_in_b"])
    y, new_shift, wkv = _mixer(z_b.reshape(bsz, t, D_PROJ_B), out_a.reshape(bsz, t, D_A), x,
                               shift0, wkv0, lw, final_g, final_norm)
    if with_vn:
        vn = vn.reshape(bsz, t, D_A)
    return y, wkv, new_shift, vn


_LAYER_PARAMS = ("norm_g", "w_in", "w_out", "sgu_ln_g", "sgu_ln_b", "sgu_w", "sgu_b", "shift_mu",
                 "w0", "w2", "a0", "a2", "k_k", "k_a", "r_k", "gn_g", "gn_b")


def kernel(x_prompt, x_sample, state_b_wkv, state_b_shift, norm_g, w_in, w_out, sgu_ln_g, sgu_ln_b,
           sgu_w, sgu_b, shift_mu, w0, w2, a0, a2, k_k, k_a, r_k, gn_g, gn_b, final_g):
    stacked = dict(zip(_LAYER_PARAMS, (norm_g, w_in, w_out, sgu_ln_g, sgu_ln_b, sgu_w, sgu_b, shift_mu,
                                       w0, w2, a0, a2, k_k, k_a, r_k, gn_g, gn_b)))
    depth = w_in.shape[0]
    bp = x_prompt.shape[0]
    zero_wkv = jnp.zeros((bp, B_HEADS, B_HEAD_DIM, B_HEAD_DIM), F32)
    zero_shift = jnp.zeros((bp, 1, N_SHIFT), F32)
    yp, ys = x_prompt, x_sample
    wkv_p, shift_p, wkv_s, shift_s, sgu_v_s = [], [], [], [], []
    for l in range(depth):
        lw = {name: p[l] for name, p in stacked.items()}
        w_in_bf16 = lw["w_in"].astype(BF16)
        lw["w_in_a"] = w_in_bf16[:, :D_PROJ_A]
        lw["w_in_b"] = w_in_bf16[:, D_PROJ_A:]
        lw["w_out"] = lw["w_out"].astype(BF16)
        last = l == depth - 1
        yp, s_p, sh_p, _ = _hybrid_layer(yp, zero_wkv, zero_shift, lw, final_g, last, False)
        ys, s_s, sh_s, vn_s = _hybrid_layer(ys, state_b_wkv[l], state_b_shift[l], lw, final_g, last, True)
        wkv_p.append(s_p); shift_p.append(sh_p)
        wkv_s.append(s_s); shift_s.append(sh_s); sgu_v_s.append(vn_s)
    return (yp, ys, jnp.stack(wkv_p), jnp.stack(shift_p), jnp.stack(wkv_s), jnp.stack(shift_s),
            jnp.stack(sgu_v_s))
```

```python
import functools
import math

import jax
import jax.numpy as jnp
from jax import lax
from jax.experimental import pallas as pl
from jax.experimental.pallas import tpu as pltpu

D_MODEL = 2048
D_A = 1024
D_B = 1024
SGU_CHUNK = 128
A_HEADS = 8
A_HEAD_DIM = D_A // A_HEADS
B_HEAD_DIM = 64
B_HEADS = D_B // B_HEAD_DIM
LORA = 64
N_SHIFT = 3 * D_B + 2 * LORA
D_PROJ_A = 3 * D_A
D_PROJ_B = N_SHIFT + D_B
NORM_EPS = 1e-6
LN_EPS = 1e-5
GN_EPS = 64e-5
EXP_NEG_W_OFFSET = math.exp(-0.5)

WKV_CHUNK = 64
HEADS_PER_GROUP = 4
GROUP_W = HEADS_PER_GROUP * B_HEAD_DIM
N_GROUPS = B_HEADS // HEADS_PER_GROUP

ROW_TILE = 512
STREAMS_PER_BLOCK = 4
CHUNKS_PER_BLOCK = 2
IN_COL_TILE = 4224
VMEM_LIMIT = 56 * 1024 * 1024

F32 = jnp.float32
BF16 = jnp.bfloat16


def _rms_norm_rows(x, g):
    ms = jnp.mean(x * x, axis=-1, keepdims=True)
    return x * lax.rsqrt(ms + NORM_EPS) * g


def _gelu(x):
    return 0.5 * x * (1.0 + lax.erf(x * (1.0 / math.sqrt(2.0))))


def _silu(x):
    return x * jax.nn.sigmoid(x)


_NN = (((1,), (0,)), ((), ()))
_NT = (((1,), (1,)), ((), ()))
_TN = (((0,), (0,)), ((), ()))


def _split_bf16(x, n):
    parts = []
    for _ in range(n - 1):
        p = x.astype(BF16)
        parts.append(p)
        x = x - p.astype(F32)
    parts.append(x.astype(BF16))
    return parts


def _dg(a, b, dims):
    return lax.dot_general(a, b, dims, preferred_element_type=F32)


def _mm(a, b, dims=_NN):
    return _dg(a.astype(BF16), b.astype(BF16), dims)


def _mm_exact_lhs(a, b, n):
    ab = a.astype(BF16)
    return sum(_dg(ab, p, _NN) for p in _split_bf16(b, n))


def _iota(shape, dim):
    return lax.broadcasted_iota(jnp.int32, shape, dim)


def _gmlp_kernel(x_ref, g_ref, w_ref, lng_ref, lnb_ref, sguw_ref, sgub_ref, outa_ref, *vn_refs,
                 rows, sgu_l):
    h = _rms_norm_rows(x_ref[...], g_ref[...]).astype(BF16)
    v = jnp.dot(h, w_ref[:, D_A:2 * D_A], preferred_element_type=F32)
    u = jnp.dot(h, w_ref[:, 0:D_A], preferred_element_type=F32)
    g_a = jnp.dot(h, w_ref[:, 2 * D_A:3 * D_A], preferred_element_type=F32)
    gv = _gelu(v)
    mean = jnp.mean(gv, axis=-1, keepdims=True)
    cen = gv - mean
    var = jnp.mean(cen * cen, axis=-1, keepdims=True)
    vn = cen * lax.rsqrt(var + LN_EPS) * lng_ref[...] + lnb_ref[...]
    if vn_refs:
        vn_refs[0][...] = vn
    tril_l = _iota((sgu_l, sgu_l), 0) >= _iota((sgu_l, sgu_l), 1)
    vn_b = vn.astype(BF16)
    f_heads = []
    for hd in range(A_HEADS):
        w = jnp.where(tril_l, sguw_ref[hd], 0.0).astype(BF16)
        cols = slice(hd * A_HEAD_DIM, (hd + 1) * A_HEAD_DIM)
        f_chunks = [jnp.dot(w, vn_b[i * sgu_l:(i + 1) * sgu_l, cols], preferred_element_type=F32)
                    for i in range(rows // sgu_l)]
        f_heads.append(jnp.concatenate(f_chunks, axis=0) if len(f_chunks) > 1 else f_chunks[0])
    f = jnp.concatenate(f_heads, axis=1) + sgub_ref[...]
    outa_ref[...] = (_gelu(u) * f * _silu(g_a)).astype(outa_ref.dtype)


def _gmlp(x2d, t, lw, with_vn):
    n_rows = x2d.shape[0]
    sgu_l = min(t, SGU_CHUNK)
    rows = ROW_TILE
    assert n_rows % rows == 0 and rows % sgu_l == 0 and t % sgu_l == 0
    row = lambda p: p.reshape(1, -1)
    sgu_w = lw["sgu_w"][:, :sgu_l, :sgu_l]
    sgu_b = jnp.tile(jnp.repeat(lw["sgu_b"][:, :sgu_l].T, A_HEAD_DIM, axis=1), (rows // sgu_l, 1))
    const = lambda shape: pl.BlockSpec(shape, lambda i: (0,) * len(shape))
    out_specs = [pl.BlockSpec((rows, D_A), lambda i: (i, 0))]
    out_shape = [jax.ShapeDtypeStruct((n_rows, D_A), BF16)]
    if with_vn:
        out_specs.append(pl.BlockSpec((rows, D_A), lambda i: (i, 0)))
        out_shape.append(jax.ShapeDtypeStruct((n_rows, D_A), F32))
    outs = pl.pallas_call(
        functools.partial(_gmlp_kernel, rows=rows, sgu_l=sgu_l),
        grid=(n_rows // rows,),
        in_specs=[
            pl.BlockSpec((rows, D_MODEL), lambda i: (i, 0)),
            const((1, D_MODEL)),
            pl.BlockSpec((D_MODEL, D_PROJ_A), lambda i: (0, 0), pipeline_mode=pl.Buffered(1)),
            const((1, D_A)), const((1, D_A)),
            const((A_HEADS, sgu_l, sgu_l)), const((rows, D_A)),
        ],
        out_specs=out_specs,
        out_shape=out_shape,
        compiler_params=pltpu.CompilerParams(
            dimension_semantics=("arbitrary",), vmem_limit_bytes=VMEM_LIMIT),
        name="gmlp",
    )(x2d, row(lw["norm_g"]), lw["w_in_a"], row(lw["sgu_ln_g"]), row(lw["sgu_ln_b"]), sgu_w, sgu_b)
    return outs[0], (outs[1] if with_vn else None)


def _proj_in_kernel(x_ref, g_ref, w_ref, z_ref):
    h = _rms_norm_rows(x_ref[...], g_ref[...])
    z_ref[...] = jnp.dot(h.astype(BF16), w_ref[...], preferred_element_type=F32)


def _proj_in(x2d, norm_g, w_in_b):
    rows = x2d.shape[0]
    assert rows % ROW_TILE == 0 and D_PROJ_B % IN_COL_TILE == 0
    return pl.pallas_call(
        _proj_in_kernel,
        grid=(D_PROJ_B // IN_COL_TILE, rows // ROW_TILE),
        in_specs=[
            pl.BlockSpec((ROW_TILE, D_MODEL), lambda j, i: (i, 0)),
            pl.BlockSpec((1, D_MODEL), lambda j, i: (0, 0)),
            pl.BlockSpec((D_MODEL, IN_COL_TILE), lambda j, i: (0, j), pipeline_mode=pl.Buffered(1)),
        ],
        out_specs=pl.BlockSpec((ROW_TILE, IN_COL_TILE), lambda j, i: (i, j)),
        out_shape=jax.ShapeDtypeStruct((rows, D_PROJ_B), F32),
        compiler_params=pltpu.CompilerParams(
            dimension_semantics=("arbitrary", "arbitrary"), vmem_limit_bytes=VMEM_LIMIT),
        name="proj_in",
    )(x2d, norm_g.reshape(1, D_MODEL), w_in_b)


def _run_interleaved(*gens):
    live = list(gens)
    while live:
        for gen in list(live):
            try:
                next(gen)
            except StopIteration:
                live.remove(gen)


def _mixer_kernel(z_ref, shift0_ref, wkv0_ref, mu_ref, wl_ref, bl_ref,
                  kk_ref, ka_ref, rk_ref, gng_ref, gnb_ref,
                  outb_ref, shift_out_ref, wkv_out_ref, s_scr, prev_scr, *, n_streams, n_chunks):
    c = WKV_CHUNK
    gw = GROUP_W
    tb = n_streams * n_chunks * c
    t_idx = pl.program_id(1)
    n_t = pl.num_programs(1)

    bd_mask = _iota((gw, gw), 0) // B_HEAD_DIM == _iota((gw, gw), 1) // B_HEAD_DIM
    group_ones = bd_mask.astype(F32)

    @pl.when(t_idx == 0)
    def _init():
        prev_scr[...] = shift0_ref[...]
        for si in range(n_streams):
            for g in range(N_GROUPS):
                s0 = wkv0_ref[si, g * gw:(g + 1) * gw, :]
                s_scr[si * N_GROUPS + g] = jnp.where(
                    bd_mask, jnp.concatenate([s0] * HEADS_PER_GROUP, axis=1), 0.0)

    def gsum(x):
        n = x.shape[0]
        stacked = jnp.concatenate([x[:, g * gw:(g + 1) * gw] for g in range(N_GROUPS)], axis=0)
        sums = _mm(stacked, group_ones)
        return jnp.concatenate([sums[g * n:(g + 1) * n] for g in range(N_GROUPS)], axis=1)

    t_w = _iota((c, gw), 0)
    s_w = _iota((c, gw), 1) % c
    strict_w = t_w > s_w
    incl_w = t_w >= s_w
    eye_w = (t_w == s_w).astype(F32)
    i_bd = _iota((gw, gw), 0) % c
    j_bd = _iota((gw, gw), 1) % c
    head_w = _iota((c, gw), 1) // B_HEAD_DIM
    first_step = _iota((c, N_SHIFT), 0) == 0
    lanes_of = lambda g: slice(g * gw, (g + 1) * gw)

    def block_diag(x):
        return jnp.where(bd_mask, jnp.concatenate([x] * HEADS_PER_GROUP, axis=0), 0.0)

    pair_w = (t_w % 2 == 1) & (s_w == t_w - 1)
    off_masks = {}
    m_ = 2
    while m_ < c:
        off_masks[m_] = bd_mask & (i_bd // (2 * m_) == j_bd // (2 * m_)) \
            & ((i_bd // m_) % 2 == 1) & ((j_bd // m_) % 2 == 0)
        m_ *= 2
    eye_b = eye_w.astype(BF16)

    ops = {}
    pre = {}
    y_of = {}

    def prep(units):
        nr = len(units) * c
        parts = []
        for si, cj in units:
            zs = z_ref[si, cj * c:(cj + 1) * c, 0:N_SHIFT]
            before = prev_scr[si] if cj == 0 else z_ref[si, cj * c - 1:cj * c, 0:N_SHIFT]
            z_prev = jnp.where(first_step, before, pltpu.roll(zs, 1, axis=0))
            parts.append(zs + mu_ref[...] * (z_prev - zs))
            if cj == n_chunks - 1:
                last_row = zs[c - 1:c, :]
                prev_scr[si] = last_row
                shift_out_ref[si] = last_row
            yield
        zmix = jnp.concatenate(parts, axis=0) if len(parts) > 1 else parts[0]
        r = zmix[:, 0:D_B]
        k = zmix[:, D_B:2 * D_B]
        vb = zmix[:, 2 * D_B:3 * D_B]
        lo = zmix[:, 3 * D_B:3 * D_B + 2 * LORA]
        lo_in = jnp.where(_iota((nr, 2 * LORA), 1) < LORA, jnp.tanh(lo), lo)
        da = _mm(lo_in, wl_ref[...]) + bl_ref[...]
        yield
        logw = -EXP_NEG_W_OFFSET * jax.nn.sigmoid(da[:, 0:D_B])
        a = jax.nn.sigmoid(da[:, D_B:2 * D_B])
        yield
        kk = k * kk_ref[...]
        norm2 = gsum(kk * kk)
        yield
        kk = kk / jnp.maximum(jnp.sqrt(norm2), 1e-12)
        k = k * (1.0 + (a - 1.0) * ka_ref[...])
        b = kk * a
        yield
        tril = _iota((nr, nr), 0) >= _iota((nr, nr), 1)
        same_chunk = _iota((nr, nr), 0) // c == _iota((nr, nr), 1) // c
        cum = _mm_exact_lhs(jnp.where(tril & same_chunk, 1.0, 0.0), logw, 2)
        bonus = gsum(r * k * rk_ref[...]) * vb
        yield
        for j, un in enumerate(units):
            rows = slice(j * c, (j + 1) * c)
            cum_c = cum[rows]
            cum_last = cum_c[c - 1:c, :]
            g_inv = jnp.exp(-cum_c)
            g_hat = jnp.exp(cum_last - cum_c)
            ops[un] = dict(
                r_t=r[rows] * jnp.exp(cum_c),
                kk_t=kk[rows] * jnp.exp(cum_c - logw[rows]),
                b_t=b[rows] * g_inv, k_t=k[rows] * g_inv,
                b_h=b[rows] * g_hat, k_h=k[rows] * g_hat,
                v=vb[rows], g_last=jnp.exp(cum_last), bonus=bonus[rows])
            yield

    def chains(units):
        keys = [(un, g) for un in units for g in range(N_GROUPS)]
        a_b, a_k, a_rb, a_rk, v_bd, l_bd, d_w = {}, {}, {}, {}, {}, {}, {}
        for ch in keys:
            un, g = ch
            lanes = lanes_of(g)
            o = ops[un]
            lhs = jnp.concatenate([o["kk_t"][:, lanes], o["r_t"][:, lanes]], axis=0)
            rb = jnp.concatenate(
                [jnp.where(head_w == h, o["b_t"][:, lanes], 0.0) for h in range(HEADS_PER_GROUP)]
                + [jnp.where(head_w == h, o["k_t"][:, lanes], 0.0) for h in range(HEADS_PER_GROUP)],
                axis=0)
            a_all = _mm(lhs, rb, _NT)
            a_b[ch], a_k[ch] = a_all[:c, :gw], a_all[:c, gw:]
            a_rb[ch], a_rk[ch] = a_all[c:, :gw], a_all[c:, gw:]
        yield
        for ch in keys:
            un, g = ch
            l_w = jnp.where(strict_w, a_b[ch], 0.0).astype(BF16)
            l_bd[ch] = jnp.concatenate([l_w] * HEADS_PER_GROUP, axis=0)
            v_bd[ch] = block_diag(ops[un]["v"][:, lanes_of(g)].astype(BF16))
            d_w[ch] = jnp.where(pair_w, -l_w, eye_b)
        yield
        m = 2
        while m < c:
            tmp = {ch: _mm(d_w[ch], jnp.where(off_masks[m], l_bd[ch], 0.0)).astype(BF16) for ch in keys}
            yield
            d_w = {ch: (d_w[ch].astype(F32) - _mm(tmp[ch], block_diag(d_w[ch]))).astype(BF16) for ch in keys}
            yield
            m *= 2
        av = {ch: _mm(jnp.concatenate([jnp.where(strict_w, a_k[ch], 0.0), jnp.where(incl_w, a_rk[ch], 0.0)],
                                      axis=0), v_bd[ch]) for ch in keys}
        yield
        du = {ch: _mm(d_w[ch], jnp.concatenate(
            [block_diag(ops[ch[0]]["kk_t"][:, lanes_of(ch[1])].astype(BF16)), block_diag(av[ch][:c].astype(BF16))],
            axis=1)) for ch in keys}
        for ch in keys:
            pre[ch] = dict(dk=du[ch][:, :gw], u0=-du[ch][:, gw:], yv=av[ch][c:], a_rb=a_rb[ch])
        yield

    def state_dep(units):
        for un in units:
            si = un[0]
            o = ops[un]
            s_old = [s_scr[si * N_GROUPS + g] for g in range(N_GROUPS)]
            pu = [_mm(jnp.concatenate([pre[(un, g)]["dk"], o["r_t"][:, lanes_of(g)]], axis=0), s_old[g], _NT)
                  for g in range(N_GROUPS)]
            u = [pre[(un, g)]["u0"] - pu[g][:c] for g in range(N_GROUPS)]
            yield
            upd = [_mm(jnp.concatenate([u[g], o["v"][:, lanes_of(g)]], axis=0),
                       jnp.concatenate([o["b_h"][:, lanes_of(g)], o["k_h"][:, lanes_of(g)]], axis=0), _TN)
                   for g in range(N_GROUPS)]
            for g in range(N_GROUPS):
                s_scr[si * N_GROUPS + g] = (s_old[g] * o["g_last"][:, lanes_of(g)]
                                            + jnp.where(bd_mask, upd[g], 0.0))
            yield
            y_groups = [pu[g][c:] + pre[(un, g)]["yv"]
                        + _mm(jnp.where(incl_w, pre[(un, g)]["a_rb"], 0.0), block_diag(u[g]))
                        for g in range(N_GROUPS)]
            y_of[un] = jnp.concatenate(y_groups, axis=1)
            yield

    half = max(n_streams // 2, 1)
    groups = [[(si, cj) for si in sis]
              for cj in range(n_chunks)
              for sis in (range(half), range(half, n_streams)) if len(sis)]
    ng = len(groups)
    for step in range(ng + 2):
        gens = []
        if 0 <= step - 2 < ng:
            gens.append(state_dep(groups[step - 2]))
        if 0 <= step - 1 < ng:
            gens.append(chains(groups[step - 1]))
        if step < ng:
            gens.append(prep(groups[step]))
        _run_interleaved(*gens)

    units_in_row_order = [(si, cj) for si in range(n_streams) for cj in range(n_chunks)]
    y = jnp.concatenate([y_of[un] for un in units_in_row_order], axis=0)
    bonus = jnp.concatenate([ops[un]["bonus"] for un in units_in_row_order], axis=0)
    mu_y = gsum(y) * (1.0 / B_HEAD_DIM)
    y_cen = y - mu_y
    var_y = gsum(y_cen * y_cen) * (1.0 / B_HEAD_DIM)
    y = y_cen * lax.rsqrt(var_y + GN_EPS) * gng_ref[...] + gnb_ref[...]
    y = y + bonus
    g_b = z_ref[:, :, N_SHIFT:D_PROJ_B].reshape(tb, D_B)
    outb_ref[...] = (y * _silu(g_b)).astype(BF16).reshape(n_streams, n_chunks * c, D_B)

    @pl.when(t_idx == n_t - 1)
    def _final():
        d = B_HEAD_DIM
        for si in range(n_streams):
            for g in range(N_GROUPS):
                s_bd = s_scr[si * N_GROUPS + g]
                wkv_out_ref[si, g * gw:(g + 1) * gw, :] = jnp.concatenate(
                    [s_bd[h * d:(h + 1) * d, h * d:(h + 1) * d] for h in range(HEADS_PER_GROUP)], axis=0)


def _mixer(z, shift0, wkv0, lw):
    bsz, t, _ = z.shape
    ns = STREAMS_PER_BLOCK
    nc = min(CHUNKS_PER_BLOCK, t // WKV_CHUNK)
    c = nc * WKV_CHUNK
    assert t % c == 0 and bsz % ns == 0
    row = lambda p: p.reshape(1, -1)
    zeros = jnp.zeros((LORA, D_B), F32)
    w_lora = jnp.concatenate([jnp.concatenate([lw["w2"], zeros], axis=1),
                              jnp.concatenate([zeros, lw["a2"]], axis=1)], axis=0)
    b_lora = jnp.concatenate([lw["w0"], lw["a0"]]).reshape(1, 2 * D_B)
    const = lambda shape: pl.BlockSpec(shape, lambda b_, t_: (0,) * len(shape))
    in_specs = [
        pl.BlockSpec((ns, c, D_PROJ_B), lambda b_, t_: (b_, t_, 0)),
        pl.BlockSpec((ns, 1, N_SHIFT), lambda b_, t_: (b_, 0, 0)),
        pl.BlockSpec((ns, D_B, B_HEAD_DIM), lambda b_, t_: (b_, 0, 0)),
        const((1, N_SHIFT)),
        const((2 * LORA, 2 * D_B)), const((1, 2 * D_B)),
        const((1, D_B)), const((1, D_B)), const((1, D_B)), const((1, D_B)), const((1, D_B)),
    ]
    out_specs = [
        pl.BlockSpec((ns, c, D_B), lambda b_, t_: (b_, t_, 0)),
        pl.BlockSpec((ns, 1, N_SHIFT), lambda b_, t_: (b_, 0, 0)),
        pl.BlockSpec((ns, D_B, B_HEAD_DIM), lambda b_, t_: (b_, 0, 0)),
    ]
    out_shape = [
        jax.ShapeDtypeStruct((bsz, t, D_B), BF16),
        jax.ShapeDtypeStruct((bsz, 1, N_SHIFT), F32),
        jax.ShapeDtypeStruct((bsz, D_B, B_HEAD_DIM), F32),
    ]
    out_b, new_shift, wkv = pl.pallas_call(
        functools.partial(_mixer_kernel, n_streams=ns, n_chunks=nc),
        grid=(bsz // ns, t // c),
        in_specs=in_specs,
        out_specs=out_specs,
        out_shape=out_shape,
        scratch_shapes=[pltpu.VMEM((ns * N_GROUPS, GROUP_W, GROUP_W), F32),
                        pltpu.VMEM((ns, 1, N_SHIFT), F32)],
        compiler_params=pltpu.CompilerParams(
            dimension_semantics=("arbitrary", "arbitrary"), vmem_limit_bytes=VMEM_LIMIT),
        name="mixer",
    )(z, shift0, wkv0.reshape(bsz, D_B, B_HEAD_DIM),
      row(lw["shift_mu"]), w_lora, b_lora,
      row(lw["k_k"]), row(lw["k_a"]), row(lw["r_k"]), row(lw["gn_g"]), row(lw["gn_b"]))
    return out_b, new_shift, wkv.reshape(bsz, B_HEADS, B_HEAD_DIM, B_HEAD_DIM)


def _proj_out_kernel(x_ref, oa_ref, ob_ref, w_ref, g_ref, y_ref, *, final_norm):
    y = (x_ref[...] + jnp.dot(oa_ref[...], w_ref[0:D_A, :], preferred_element_type=F32)
         + jnp.dot(ob_ref[...], w_ref[D_A:D_A + D_B, :], preferred_element_type=F32))
    if final_norm:
        y = _rms_norm_rows(y, g_ref[...])
    y_ref[...] = y


def _proj_out(x2d, out_a, out_b, w_out, final_g, final_norm):
    rows = x2d.shape[0]
    assert rows % ROW_TILE == 0
    return pl.pallas_call(
        functools.partial(_proj_out_kernel, final_norm=final_norm),
        grid=(rows // ROW_TILE,),
        in_specs=[
            pl.BlockSpec((ROW_TILE, D_MODEL), lambda i: (i, 0)),
            pl.BlockSpec((ROW_TILE, D_A), lambda i: (i, 0)),
            pl.BlockSpec((ROW_TILE, D_B), lambda i: (i, 0)),
            pl.BlockSpec((D_A + D_B, D_MODEL), lambda i: (0, 0), pipeline_mode=pl.Buffered(1)),
            pl.BlockSpec((1, D_MODEL), lambda i: (0, 0)),
        ],
        out_specs=pl.BlockSpec((ROW_TILE, D_MODEL), lambda i: (i, 0)),
        out_shape=jax.ShapeDtypeStruct((rows, D_MODEL), F32),
        compiler_params=pltpu.CompilerParams(
            dimension_semantics=("arbitrary",), vmem_limit_bytes=VMEM_LIMIT),
        name="proj_out",
    )(x2d, out_a, out_b, w_out, final_g.reshape(1, D_MODEL))


def _hybrid_layer(x, wkv0, shift0, lw, final_g, final_norm, with_vn):
    bsz, t, _ = x.shape
    x2d = x.reshape(bsz * t, D_MODEL)
    out_a, vn = _gmlp(x2d, t, lw, with_vn)
    z_b = _proj_in(x2d, lw["norm_g"], lw["w_in_b"])
    out_b, new_shift, wkv = _mixer(z_b.reshape(bsz, t, D_PROJ_B), shift0, wkv0, lw)
    y = _proj_out(x2d, out_a, out_b.reshape(bsz * t, D_B), lw["w_out"], final_g, final_norm).reshape(bsz, t, D_MODEL)
    if with_vn:
        vn = vn.reshape(bsz, t, D_A)
    return y, wkv, new_shift, vn


_LAYER_PARAMS = ("norm_g", "w_in", "w_out", "sgu_ln_g", "sgu_ln_b", "sgu_w", "sgu_b", "shift_mu",
                 "w0", "w2", "a0", "a2", "k_k", "k_a", "r_k", "gn_g", "gn_b")


def kernel(x_prompt, x_sample, state_b_wkv, state_b_shift, norm_g, w_in, w_out, sgu_ln_g, sgu_ln_b,
           sgu_w, sgu_b, shift_mu, w0, w2, a0, a2, k_k, k_a, r_k, gn_g, gn_b, final_g):
    stacked = dict(zip(_LAYER_PARAMS, (norm_g, w_in, w_out, sgu_ln_g, sgu_ln_b, sgu_w, sgu_b, shift_mu,
                                       w0, w2, a0, a2, k_k, k_a, r_k, gn_g, gn_b)))
    depth = w_in.shape[0]
    bp = x_prompt.shape[0]
    zero_wkv = jnp.zeros((bp, B_HEADS, B_HEAD_DIM, B_HEAD_DIM), F32)
    zero_shift = jnp.zeros((bp, 1, N_SHIFT), F32)
    yp, ys = x_prompt, x_sample
    wkv_p, shift_p, wkv_s, shift_s, sgu_v_s = [], [], [], [], []
    for l in range(depth):
        lw = {name: p[l] for name, p in stacked.items()}
        w_in_bf16 = lw["w_in"].astype(BF16)
        lw["w_in_a"] = w_in_bf16[:, :D_PROJ_A]
        lw["w_in_b"] = w_in_bf16[:, D_PROJ_A:]
        lw["w_out"] = lw["w_out"].astype(BF16)
        last = l == depth - 1
        yp, s_p, sh_p, _ = _hybrid_layer(yp, zero_wkv, zero_shift, lw, final_g, last, False)
        ys, s_s, sh_s, vn_s = _hybrid_layer(ys, state_b_wkv[l], state_b_shift[l], lw, final_g, last, True)
        wkv_p.append(s_p); shift_p.append(sh_p)
        wkv_s.append(s_s); shift_s.append(sh_s); sgu_v_s.append(vn_s)
    return (yp, ys, jnp.stack(wkv_p), jnp.stack(shift_p), jnp.stack(wkv_s), jnp.stack(shift_s),
            jnp.stack(sgu_v_s))
```

```python
import functools
import math

import jax
import jax.numpy as jnp
from jax import lax
from jax.experimental import pallas as pl
from jax.experimental.pallas import tpu as pltpu

D_MODEL = 2048
D_A = 1024
D_B = 1024
SGU_CHUNK = 128
A_HEADS = 8
A_HEAD_DIM = D_A // A_HEADS
B_HEAD_DIM = 64
B_HEADS = D_B // B_HEAD_DIM
LORA = 64
N_SHIFT = 3 * D_B + 2 * LORA
D_PROJ_A = 3 * D_A
D_PROJ_B = N_SHIFT + D_B
NORM_EPS = 1e-6
LN_EPS = 1e-5
GN_EPS = 64e-5
EXP_NEG_W_OFFSET = math.exp(-0.5)

WKV_CHUNK = 64
HEADS_PER_GROUP = 4
GROUP_W = HEADS_PER_GROUP * B_HEAD_DIM
N_GROUPS = B_HEADS // HEADS_PER_GROUP

ROW_TILE = 512
STREAMS_PER_BLOCK = 4
CHUNKS_PER_BLOCK = 2
IN_COL_TILE = 4224
VMEM_LIMIT = 56 * 1024 * 1024

F32 = jnp.float32
BF16 = jnp.bfloat16


def _rms_norm_rows(x, g):
    ms = jnp.mean(x * x, axis=-1, keepdims=True)
    return x * lax.rsqrt(ms + NORM_EPS) * g


def _gelu(x):
    return 0.5 * x * (1.0 + lax.erf(x * (1.0 / math.sqrt(2.0))))


def _silu(x):
    return x * jax.nn.sigmoid(x)


_NN = (((1,), (0,)), ((), ()))
_NT = (((1,), (1,)), ((), ()))
_TN = (((0,), (0,)), ((), ()))


def _split_bf16(x, n):
    parts = []
    for _ in range(n - 1):
        p = x.astype(BF16)
        parts.append(p)
        x = x - p.astype(F32)
    parts.append(x.astype(BF16))
    return parts


def _dg(a, b, dims):
    return lax.dot_general(a, b, dims, preferred_element_type=F32)


def _mm(a, b, dims=_NN):
    return _dg(a.astype(BF16), b.astype(BF16), dims)


def _mm_exact_lhs(a, b, n):
    ab = a.astype(BF16)
    return sum(_dg(ab, p, _NN) for p in _split_bf16(b, n))


def _iota(shape, dim):
    return lax.broadcasted_iota(jnp.int32, shape, dim)


def _gmlp_kernel(x_ref, g_ref, w_ref, lng_ref, lnb_ref, sguw_ref, sgub_ref, outa_ref, *vn_refs,
                 rows, sgu_l):
    h = _rms_norm_rows(x_ref[...], g_ref[...]).astype(BF16)
    v = jnp.dot(h, w_ref[:, D_A:2 * D_A], preferred_element_type=F32)
    u = jnp.dot(h, w_ref[:, 0:D_A], preferred_element_type=F32)
    g_a = jnp.dot(h, w_ref[:, 2 * D_A:3 * D_A], preferred_element_type=F32)
    gv = _gelu(v)
    mean = jnp.mean(gv, axis=-1, keepdims=True)
    cen = gv - mean
    var = jnp.mean(cen * cen, axis=-1, keepdims=True)
    vn = cen * lax.rsqrt(var + LN_EPS) * lng_ref[...] + lnb_ref[...]
    if vn_refs:
        vn_refs[0][...] = vn
    tril_l = _iota((sgu_l, sgu_l), 0) >= _iota((sgu_l, sgu_l), 1)
    vn_b = vn.astype(BF16)
    f_heads = []
    for hd in range(A_HEADS):
        w = jnp.where(tril_l, sguw_ref[hd], 0.0).astype(BF16)
        cols = slice(hd * A_HEAD_DIM, (hd + 1) * A_HEAD_DIM)
        f_chunks = [jnp.dot(w, vn_b[i * sgu_l:(i + 1) * sgu_l, cols], preferred_element_type=F32)
                    for i in range(rows // sgu_l)]
        f_heads.append(jnp.concatenate(f_chunks, axis=0) if len(f_chunks) > 1 else f_chunks[0])
    f = jnp.concatenate(f_heads, axis=1) + sgub_ref[...]
    outa_ref[...] = (_gelu(u) * f * _silu(g_a)).astype(outa_ref.dtype)


def _gmlp(x2d, t, lw, with_vn):
    n_rows = x2d.shape[0]
    sgu_l = min(t, SGU_CHUNK)
    rows = ROW_TILE
    assert n_rows % rows == 0 and rows % sgu_l == 0 and t % sgu_l == 0
    row = lambda p: p.reshape(1, -1)
    sgu_w = lw["sgu_w"][:, :sgu_l, :sgu_l]
    sgu_b = jnp.tile(jnp.repeat(lw["sgu_b"][:, :sgu_l].T, A_HEAD_DIM, axis=1), (rows // sgu_l, 1))
    const = lambda shape: pl.BlockSpec(shape, lambda i: (0,) * len(shape))
    out_specs = [pl.BlockSpec((rows, D_A), lambda i: (i, 0))]
    out_shape = [jax.ShapeDtypeStruct((n_rows, D_A), BF16)]
    if with_vn:
        out_specs.append(pl.BlockSpec((rows, D_A), lambda i: (i, 0)))
        out_shape.append(jax.ShapeDtypeStruct((n_rows, D_A), F32))
    outs = pl.pallas_call(
        functools.partial(_gmlp_kernel, rows=rows, sgu_l=sgu_l),
        grid=(n_rows // rows,),
        in_specs=[
            pl.BlockSpec((rows, D_MODEL), lambda i: (i, 0)),
            const((1, D_MODEL)),
            pl.BlockSpec((D_MODEL, D_PROJ_A), lambda i: (0, 0), pipeline_mode=pl.Buffered(1)),
            const((1, D_A)), const((1, D_A)),
            const((A_HEADS, sgu_l, sgu_l)), const((rows, D_A)),
        ],
        out_specs=out_specs,
        out_shape=out_shape,
        compiler_params=pltpu.CompilerParams(
            dimension_semantics=("arbitrary",), vmem_limit_bytes=VMEM_LIMIT),
        name="gmlp",
    )(x2d, row(lw["norm_g"]), lw["w_in"], row(lw["sgu_ln_g"]), row(lw["sgu_ln_b"]), sgu_w, sgu_b)
    return outs[0], (outs[1] if with_vn else None)


def _proj_in_kernel(x_ref, g_ref, w_ref, z_ref):
    h = _rms_norm_rows(x_ref[...], g_ref[...])
    z_ref[...] = jnp.dot(h.astype(BF16), w_ref[...], preferred_element_type=F32)


def _proj_in(x2d, norm_g, w_in_b):
    rows = x2d.shape[0]
    assert rows % ROW_TILE == 0 and D_PROJ_B % IN_COL_TILE == 0
    return pl.pallas_call(
        _proj_in_kernel,
        grid=(D_PROJ_B // IN_COL_TILE, rows // ROW_TILE),
        in_specs=[
            pl.BlockSpec((ROW_TILE, D_MODEL), lambda j, i: (i, 0)),
            pl.BlockSpec((1, D_MODEL), lambda j, i: (0, 0)),
            pl.BlockSpec((D_MODEL, IN_COL_TILE), lambda j, i: (0, j), pipeline_mode=pl.Buffered(1)),
        ],
        out_specs=pl.BlockSpec((ROW_TILE, IN_COL_TILE), lambda j, i: (i, j)),
        out_shape=jax.ShapeDtypeStruct((rows, D_PROJ_B), F32),
        compiler_params=pltpu.CompilerParams(
            dimension_semantics=("arbitrary", "arbitrary"), vmem_limit_bytes=VMEM_LIMIT),
        name="proj_in",
    )(x2d, norm_g.reshape(1, D_MODEL), w_in_b)


def _run_interleaved(*gens):
    live = list(gens)
    while live:
        for gen in list(live):
            try:
                next(gen)
            except StopIteration:
                live.remove(gen)


def _mixer_kernel(z_ref, shift0_ref, wkv0_ref, mu_ref, wl_ref, bl_ref,
                  kk_ref, ka_ref, rk_ref, gng_ref, gnb_ref,
                  outb_ref, shift_out_ref, wkv_out_ref, s_scr, prev_scr, *, n_streams, n_chunks):
    c = WKV_CHUNK
    gw = GROUP_W
    tb = n_streams * n_chunks * c
    t_idx = pl.program_id(1)
    n_t = pl.num_programs(1)

    bd_mask = _iota((gw, gw), 0) // B_HEAD_DIM == _iota((gw, gw), 1) // B_HEAD_DIM
    group_ones = bd_mask.astype(F32)

    @pl.when(t_idx == 0)
    def _init():
        prev_scr[...] = shift0_ref[...]
        for si in range(n_streams):
            for g in range(N_GROUPS):
                s0 = wkv0_ref[si, g * gw:(g + 1) * gw, :]
                s_scr[si * N_GROUPS + g] = jnp.where(
                    bd_mask, jnp.concatenate([s0] * HEADS_PER_GROUP, axis=1), 0.0)

    def gsum(x):
        n = x.shape[0]
        stacked = jnp.concatenate([x[:, g * gw:(g + 1) * gw] for g in range(N_GROUPS)], axis=0)
        sums = _mm(stacked, group_ones)
        return jnp.concatenate([sums[g * n:(g + 1) * n] for g in range(N_GROUPS)], axis=1)

    t_w = _iota((c, gw), 0)
    s_w = _iota((c, gw), 1) % c
    strict_w = t_w > s_w
    incl_w = t_w >= s_w
    eye_w = (t_w == s_w).astype(F32)
    i_bd = _iota((gw, gw), 0) % c
    j_bd = _iota((gw, gw), 1) % c
    head_w = _iota((c, gw), 1) // B_HEAD_DIM
    first_step = _iota((c, N_SHIFT), 0) == 0
    lanes_of = lambda g: slice(g * gw, (g + 1) * gw)

    def block_diag(x):
        return jnp.where(bd_mask, jnp.concatenate([x] * HEADS_PER_GROUP, axis=0), 0.0)

    pair_w = (t_w % 2 == 1) & (s_w == t_w - 1)
    off_masks = {}
    m_ = 2
    while m_ < c:
        off_masks[m_] = bd_mask & (i_bd // (2 * m_) == j_bd // (2 * m_)) \
            & ((i_bd // m_) % 2 == 1) & ((j_bd // m_) % 2 == 0)
        m_ *= 2
    eye_b = eye_w.astype(BF16)

    ops = {}
    pre = {}
    y_of = {}

    def prep(units):
        nr = len(units) * c
        parts = []
        for si, cj in units:
            zs = z_ref[si, cj * c:(cj + 1) * c, 0:N_SHIFT]
            before = prev_scr[si] if cj == 0 else z_ref[si, cj * c - 1:cj * c, 0:N_SHIFT]
            z_prev = jnp.where(first_step, before, pltpu.roll(zs, 1, axis=0))
            parts.append(zs + mu_ref[...] * (z_prev - zs))
            if cj == n_chunks - 1:
                last_row = zs[c - 1:c, :]
                prev_scr[si] = last_row
                shift_out_ref[si] = last_row
            yield
        zmix = jnp.concatenate(parts, axis=0) if len(parts) > 1 else parts[0]
        r = zmix[:, 0:D_B]
        k = zmix[:, D_B:2 * D_B]
        vb = zmix[:, 2 * D_B:3 * D_B]
        lo = zmix[:, 3 * D_B:3 * D_B + 2 * LORA]
        lo_in = jnp.where(_iota((nr, 2 * LORA), 1) < LORA, jnp.tanh(lo), lo)
        da = _mm(lo_in, wl_ref[...]) + bl_ref[...]
        yield
        logw = -EXP_NEG_W_OFFSET * jax.nn.sigmoid(da[:, 0:D_B])
        a = jax.nn.sigmoid(da[:, D_B:2 * D_B])
        yield
        kk = k * kk_ref[...]
        norm2 = gsum(kk * kk)
        yield
        kk = kk * lax.rsqrt(jnp.maximum(norm2, 1e-24))
        k = k * (1.0 + (a - 1.0) * ka_ref[...])
        b = kk * a
        yield
        tril = _iota((nr, nr), 0) >= _iota((nr, nr), 1)
        same_chunk = _iota((nr, nr), 0) // c == _iota((nr, nr), 1) // c
        cum = _mm_exact_lhs(jnp.where(tril & same_chunk, 1.0, 0.0), logw, 2)
        bonus = gsum(r * k * rk_ref[...]) * vb
        yield
        for j, un in enumerate(units):
            rows = slice(j * c, (j + 1) * c)
            cum_c = cum[rows]
            cum_last = cum_c[c - 1:c, :]
            g_inv = jnp.exp(-cum_c)
            g_hat = jnp.exp(cum_last - cum_c)
            ops[un] = dict(
                r_t=r[rows] * jnp.exp(cum_c),
                kk_t=kk[rows] * jnp.exp(cum_c - logw[rows]),
                b_t=b[rows] * g_inv, k_t=k[rows] * g_inv,
                b_h=b[rows] * g_hat, k_h=k[rows] * g_hat,
                v=vb[rows], g_last=jnp.exp(cum_last), bonus=bonus[rows])
            yield

    def chains(units):
        keys = [(un, g) for un in units for g in range(N_GROUPS)]
        a_b, a_k, a_rb, a_rk, v_bd, l_bd, d_w = {}, {}, {}, {}, {}, {}, {}
        for ch in keys:
            un, g = ch
            lanes = lanes_of(g)
            o = ops[un]
            lhs = jnp.concatenate([o["kk_t"][:, lanes], o["r_t"][:, lanes]], axis=0)
            rb = jnp.concatenate(
                [jnp.where(head_w == h, o["b_t"][:, lanes], 0.0) for h in range(HEADS_PER_GROUP)]
                + [jnp.where(head_w == h, o["k_t"][:, lanes], 0.0) for h in range(HEADS_PER_GROUP)],
                axis=0)
            a_all = _mm(lhs, rb, _NT)
            a_b[ch], a_k[ch] = a_all[:c, :gw], a_all[:c, gw:]
            a_rb[ch], a_rk[ch] = a_all[c:, :gw], a_all[c:, gw:]
        yield
        for ch in keys:
            un, g = ch
            l_w = jnp.where(strict_w, a_b[ch], 0.0).astype(BF16)
            l_bd[ch] = jnp.concatenate([l_w] * HEADS_PER_GROUP, axis=0)
            v_bd[ch] = block_diag(ops[un]["v"][:, lanes_of(g)].astype(BF16))
            d_w[ch] = jnp.where(pair_w, -l_w, eye_b)
        yield
        m = 2
        while m < c:
            tmp = {ch: _mm(d_w[ch], jnp.where(off_masks[m], l_bd[ch], 0.0)).astype(BF16) for ch in keys}
            yield
            d_w = {ch: (d_w[ch].astype(F32) - _mm(tmp[ch], block_diag(d_w[ch]))).astype(BF16) for ch in keys}
            yield
            m *= 2
        av = {ch: _mm(jnp.concatenate([jnp.where(strict_w, a_k[ch], 0.0), jnp.where(incl_w, a_rk[ch], 0.0)],
                                      axis=0), v_bd[ch]) for ch in keys}
        yield
        du = {ch: _mm(d_w[ch], jnp.concatenate(
            [block_diag(ops[ch[0]]["kk_t"][:, lanes_of(ch[1])].astype(BF16)), block_diag(av[ch][:c].astype(BF16))],
            axis=1)) for ch in keys}
        for ch in keys:
            pre[ch] = dict(dk=du[ch][:, :gw], u0=-du[ch][:, gw:], yv=av[ch][c:], a_rb=a_rb[ch])
        yield

    def state_dep(units):
        for un in units:
            si = un[0]
            o = ops[un]
            s_old = [s_scr[si * N_GROUPS + g] for g in range(N_GROUPS)]
            pu = [_mm(jnp.concatenate([pre[(un, g)]["dk"], o["r_t"][:, lanes_of(g)]], axis=0), s_old[g], _NT)
                  for g in range(N_GROUPS)]
            u = [pre[(un, g)]["u0"] - pu[g][:c] for g in range(N_GROUPS)]
            yield
            upd = [_mm(jnp.concatenate([u[g], o["v"][:, lanes_of(g)]], axis=0),
                       jnp.concatenate([o["b_h"][:, lanes_of(g)], o["k_h"][:, lanes_of(g)]], axis=0), _TN)
                   for g in range(N_GROUPS)]
            for g in range(N_GROUPS):
                s_scr[si * N_GROUPS + g] = (s_old[g] * o["g_last"][:, lanes_of(g)]
                                            + jnp.where(bd_mask, upd[g], 0.0))
            yield
            y_groups = [pu[g][c:] + pre[(un, g)]["yv"]
                        + _mm(jnp.where(incl_w, pre[(un, g)]["a_rb"], 0.0), block_diag(u[g]))
                        for g in range(N_GROUPS)]
            y_of[un] = jnp.concatenate(y_groups, axis=1)
            yield

    half = max(n_streams // 2, 1)
    groups = [[(si, cj) for si in sis]
              for cj in range(n_chunks)
              for sis in (range(half), range(half, n_streams)) if len(sis)]
    ng = len(groups)
    for step in range(ng + 2):
        gens = []
        if 0 <= step - 2 < ng:
            gens.append(state_dep(groups[step - 2]))
        if 0 <= step - 1 < ng:
            gens.append(chains(groups[step - 1]))
        if step < ng:
            gens.append(prep(groups[step]))
        _run_interleaved(*gens)

    units_in_row_order = [(si, cj) for si in range(n_streams) for cj in range(n_chunks)]
    y = jnp.concatenate([y_of[un] for un in units_in_row_order], axis=0)
    bonus = jnp.concatenate([ops[un]["bonus"] for un in units_in_row_order], axis=0)
    mu_y = gsum(y) * (1.0 / B_HEAD_DIM)
    y_cen = y - mu_y
    var_y = gsum(y_cen * y_cen) * (1.0 / B_HEAD_DIM)
    y = y_cen * lax.rsqrt(var_y + GN_EPS) * gng_ref[...] + gnb_ref[...]
    y = y + bonus
    g_b = z_ref[:, :, N_SHIFT:D_PROJ_B].reshape(tb, D_B)
    outb_ref[...] = (y * _silu(g_b)).astype(BF16).reshape(n_streams, n_chunks * c, D_B)

    @pl.when(t_idx == n_t - 1)
    def _final():
        d = B_HEAD_DIM
        for si in range(n_streams):
            for g in range(N_GROUPS):
                s_bd = s_scr[si * N_GROUPS + g]
                wkv_out_ref[si, g * gw:(g + 1) * gw, :] = jnp.concatenate(
                    [s_bd[h * d:(h + 1) * d, h * d:(h + 1) * d] for h in range(HEADS_PER_GROUP)], axis=0)


def _mixer(z, shift0, wkv0, lw):
    bsz, t, _ = z.shape
    ns = STREAMS_PER_BLOCK
    nc = min(CHUNKS_PER_BLOCK, t // WKV_CHUNK)
    c = nc * WKV_CHUNK
    assert t % c == 0 and bsz % ns == 0
    row = lambda p: p.reshape(1, -1)
    zeros = jnp.zeros((LORA, D_B), F32)
    w_lora = jnp.concatenate([jnp.concatenate([lw["w2"], zeros], axis=1),
                              jnp.concatenate([zeros, lw["a2"]], axis=1)], axis=0)
    b_lora = jnp.concatenate([lw["w0"], lw["a0"]]).reshape(1, 2 * D_B)
    const = lambda shape: pl.BlockSpec(shape, lambda b_, t_: (0,) * len(shape))
    in_specs = [
        pl.BlockSpec((ns, c, D_PROJ_B), lambda b_, t_: (b_, t_, 0)),
        pl.BlockSpec((ns, 1, N_SHIFT), lambda b_, t_: (b_, 0, 0)),
        pl.BlockSpec((ns, D_B, B_HEAD_DIM), lambda b_, t_: (b_, 0, 0)),
        const((1, N_SHIFT)),
        const((2 * LORA, 2 * D_B)), const((1, 2 * D_B)),
        const((1, D_B)), const((1, D_B)), const((1, D_B)), const((1, D_B)), const((1, D_B)),
    ]
    out_specs = [
        pl.BlockSpec((ns, c, D_B), lambda b_, t_: (b_, t_, 0)),
        pl.BlockSpec((ns, 1, N_SHIFT), lambda b_, t_: (b_, 0, 0)),
        pl.BlockSpec((ns, D_B, B_HEAD_DIM), lambda b_, t_: (b_, 0, 0)),
    ]
    out_shape = [
        jax.ShapeDtypeStruct((bsz, t, D_B), BF16),
        jax.ShapeDtypeStruct((bsz, 1, N_SHIFT), F32),
        jax.ShapeDtypeStruct((bsz, D_B, B_HEAD_DIM), F32),
    ]
    out_b, new_shift, wkv = pl.pallas_call(
        functools.partial(_mixer_kernel, n_streams=ns, n_chunks=nc),
        grid=(bsz // ns, t // c),
        in_specs=in_specs,
        out_specs=out_specs,
        out_shape=out_shape,
        scratch_shapes=[pltpu.VMEM((ns * N_GROUPS, GROUP_W, GROUP_W), F32),
                        pltpu.VMEM((ns, 1, N_SHIFT), F32)],
        compiler_params=pltpu.CompilerParams(
            dimension_semantics=("arbitrary", "arbitrary"), vmem_limit_bytes=VMEM_LIMIT),
        name="mixer",
    )(z, shift0, wkv0.reshape(bsz, D_B, B_HEAD_DIM),
      row(lw["shift_mu"]), w_lora, b_lora,
      row(lw["k_k"]), row(lw["k_a"]), row(lw["r_k"]), row(lw["gn_g"]), row(lw["gn_b"]))
    return out_b, new_shift, wkv.reshape(bsz, B_HEADS, B_HEAD_DIM, B_HEAD_DIM)


def _proj_out_kernel(x_ref, oa_ref, ob_ref, w_ref, g_ref, y_ref, *, final_norm):
    y = (x_ref[...] + jnp.dot(oa_ref[...], w_ref[0:D_A, :], preferred_element_type=F32)
         + jnp.dot(ob_ref[...], w_ref[D_A:D_A + D_B, :], preferred_element_type=F32))
    if final_norm:
        y = _rms_norm_rows(y, g_ref[...])
    y_ref[...] = y


def _proj_out(x2d, out_a, out_b, w_out, final_g, final_norm):
    rows = x2d.shape[0]
    assert rows % ROW_TILE == 0
    return pl.pallas_call(
        functools.partial(_proj_out_kernel, final_norm=final_norm),
        grid=(rows // ROW_TILE,),
        in_specs=[
            pl.BlockSpec((ROW_TILE, D_MODEL), lambda i: (i, 0)),
            pl.BlockSpec((ROW_TILE, D_A), lambda i: (i, 0)),
            pl.BlockSpec((ROW_TILE, D_B), lambda i: (i, 0)),
            pl.BlockSpec((D_A + D_B, D_MODEL), lambda i: (0, 0), pipeline_mode=pl.Buffered(1)),
            pl.BlockSpec((1, D_MODEL), lambda i: (0, 0)),
        ],
        out_specs=pl.BlockSpec((ROW_TILE, D_MODEL), lambda i: (i, 0)),
        out_shape=jax.ShapeDtypeStruct((rows, D_MODEL), F32),
        compiler_params=pltpu.CompilerParams(
            dimension_semantics=("arbitrary",), vmem_limit_bytes=VMEM_LIMIT),
        name="proj_out",
    )(x2d, out_a, out_b, w_out, final_g.reshape(1, D_MODEL))


def _hybrid_layer(x, wkv0, shift0, lw, final_g, final_norm, with_vn):
    bsz, t, _ = x.shape
    x2d = x.reshape(bsz * t, D_MODEL)
    out_a, vn = _gmlp(x2d, t, lw, with_vn)
    z_b = _proj_in(x2d, lw["norm_g"], lw["w_in_b"])
    out_b, new_shift, wkv = _mixer(z_b.reshape(bsz, t, D_PROJ_B), shift0, wkv0, lw)
    y = _proj_out(x2d, out_a, out_b.reshape(bsz * t, D_B), lw["w_out"], final_g, final_norm).reshape(bsz, t, D_MODEL)
    if with_vn:
        vn = vn.reshape(bsz, t, D_A)
    return y, wkv, new_shift, vn


_LAYER_PARAMS = ("norm_g", "w_in", "w_out", "sgu_ln_g", "sgu_ln_b", "sgu_w", "sgu_b", "shift_mu",
                 "w0", "w2", "a0", "a2", "k_k", "k_a", "r_k", "gn_g", "gn_b")


def kernel(x_prompt, x_sample, state_b_wkv, state_b_shift, norm_g, w_in, w_out, sgu_ln_g, sgu_ln_b,
           sgu_w, sgu_b, shift_mu, w0, w2, a0, a2, k_k, k_a, r_k, gn_g, gn_b, final_g):
    stacked = dict(zip(_LAYER_PARAMS, (norm_g, w_in, w_out, sgu_ln_g, sgu_ln_b, sgu_w, sgu_b, shift_mu,
                                       w0, w2, a0, a2, k_k, k_a, r_k, gn_g, gn_b)))
    depth = w_in.shape[0]
    bp = x_prompt.shape[0]
    zero_wkv = jnp.zeros((bp, B_HEADS, B_HEAD_DIM, B_HEAD_DIM), F32)
    zero_shift = jnp.zeros((bp, 1, N_SHIFT), F32)
    yp, ys = x_prompt, x_sample
    wkv_p, shift_p, wkv_s, shift_s, sgu_v_s = [], [], [], [], []
    for l in range(depth):
        lw = {name: p[l] for name, p in stacked.items()}
        lw["w_in"] = lw["w_in"].astype(BF16)
        lw["w_in_b"] = lw["w_in"][:, D_PROJ_A:]
        lw["w_out"] = lw["w_out"].astype(BF16)
        last = l == depth - 1
        yp, s_p, sh_p, _ = _hybrid_layer(yp, zero_wkv, zero_shift, lw, final_g, last, False)
        ys, s_s, sh_s, vn_s = _hybrid_layer(ys, state_b_wkv[l], state_b_shift[l], lw, final_g, last, True)
        wkv_p.append(s_p); shift_p.append(sh_p)
        wkv_s.append(s_s); shift_s.append(sh_s); sgu_v_s.append(vn_s)
    return (yp, ys, jnp.stack(wkv_p), jnp.stack(shift_p), jnp.stack(wkv_s), jnp.stack(shift_s),
            jnp.stack(sgu_v_s))
```
